```python
import jax, jax.numpy as jnp
from jax import lax
import numpy as np

D_MODEL = 2048
BATCH = 4
SEQ = 8192
DEPTH = 4
DEC_BATCH = 1
DEC_SEQ = 16384
PAST_LEN = 128

HEAD_DIM = 128
SWA_HEADS = 8
SWA_KV_HEADS = 2
WINDOW = 128
BLOCK = 128
MLA_HEADS = 8
QK_NOPE = 128
QK_ROPE = 64
V_DIM = 128
Q_LORA = 512
KV_LORA = 512
ROPE_THETA = 10000.0
SWA_WIDTH = SWA_HEADS * HEAD_DIM
SWA_KV_WIDTH = SWA_KV_HEADS * HEAD_DIM
MLA_WIDTH = MLA_HEADS * V_DIM
MIX_WIDTH = SWA_WIDTH + MLA_WIDTH
SPLIT_SIZES = (SWA_WIDTH, SWA_KV_WIDTH, SWA_KV_WIDTH, Q_LORA, KV_LORA, QK_ROPE)
IN_WIDTH = sum(SPLIT_SIZES)
N_EXPERTS = 16
D_EXPERT = 1408
CAPACITY_FACTOR = 2
EPS = 1e-6

kernel_name = 'hymba_swa_mla_expert_choice_encoder'


def rmsnorm(x, g):
    xf = x.astype(jnp.float32)
    y = xf * lax.rsqrt(jnp.mean(xf * xf, axis=-1, keepdims=True) + EPS)
    return (y * g.astype(jnp.float32)).astype(x.dtype)


def alibi_slopes(n):
    return jnp.asarray(np.array([2.0 ** (-8.0 * (i + 1) / n) for i in range(n)], dtype=np.float32))


def rope_tables(s):
    inv = 1.0 / (ROPE_THETA ** (jnp.arange(0, QK_ROPE, 2, dtype=jnp.float32) / QK_ROPE))
    ang = jnp.arange(s, dtype=jnp.float32)[:, None] * inv[None, :]
    return jnp.cos(ang), jnp.sin(ang)


def apply_rope(x, cos, sin):
    xf = x.astype(jnp.float32)
    x1, x2 = jnp.split(xf, 2, axis=-1)
    return jnp.concatenate([x1 * cos - x2 * sin, x1 * sin + x2 * cos], axis=-1).astype(x.dtype)


def window_gqa(q, k, v, sink, slopes):
    b, s = q.shape[:2]
    nb = s // BLOCK
    g = SWA_HEADS // SWA_KV_HEADS
    qb = q.reshape(b, nb, BLOCK, SWA_KV_HEADS, g, HEAD_DIM)

    def band(t):
        tp = jnp.pad(t, ((0, 0), (BLOCK, BLOCK), (0, 0), (0, 0))).reshape(b, nb + 2, BLOCK, SWA_KV_HEADS, HEAD_DIM)
        return jnp.concatenate([tp[:, :-2], tp[:, 1:-1], tp[:, 2:]], axis=2)

    kb, vb = band(k), band(v)
    logits = jnp.einsum('bnqhgd,bnkhd->bnhgqk', qb, kb).astype(jnp.float32) * (HEAD_DIM ** -0.5)
    qi = jnp.arange(BLOCK)[:, None]
    kj = jnp.arange(3 * BLOCK)[None, :]
    rel = kj - BLOCK - qi
    dist = jnp.abs(rel).astype(jnp.float32)
    spos = jnp.arange(nb)[:, None, None] * BLOCK + (kj - BLOCK)[None]
    valid = (jnp.abs(rel) <= WINDOW)[None] & (spos >= 0) & (spos < s)
    bias = -slopes.reshape(SWA_KV_HEADS, g)[:, :, None, None] * dist
    logits = jnp.where(valid[None, :, None, None], logits + bias, -jnp.inf)
    sk = sink.astype(jnp.float32).reshape(SWA_KV_HEADS, g)[:, :, None, None]
    m = jnp.maximum(logits.max(axis=-1, keepdims=True), sk)
    p = jnp.exp(logits - m)
    p = p / (p.sum(axis=-1, keepdims=True) + jnp.exp(sk - m))
    o = jnp.einsum('bnhgqk,bnkhd->bnqhgd', p.astype(v.dtype), vb)
    return o.reshape(b, s, SWA_WIDTH)


def mla(c_q, c_kv, k_r, g_cq, w_uq, g_ckv, w_uk, w_uv, cos, sin):
    b, s = c_q.shape[:2]
    nb = s // BLOCK
    q = (rmsnorm(c_q, g_cq) @ w_uq).reshape(b, s, MLA_HEADS, QK_NOPE + QK_ROPE)
    q_nope = q[..., :QK_NOPE]
    q_rope = apply_rope(q[..., QK_NOPE:], cos[:, None, :], sin[:, None, :])
    ckv = rmsnorm(c_kv, g_ckv)
    k_nope = (ckv @ w_uk).reshape(b, s, MLA_HEADS, QK_NOPE)
    v = (ckv @ w_uv).reshape(b, s, MLA_HEADS, V_DIM)
    k_rope = apply_rope(k_r, cos, sin)
    scale = (QK_NOPE + QK_ROPE) ** -0.5

    def block(qs):
        qn, qr = qs
        logits = (jnp.einsum('bqhd,bkhd->bhqk', qn, k_nope)
                  + jnp.einsum('bqhr,bkr->bhqk', qr, k_rope)).astype(jnp.float32) * scale
        p = jax.nn.softmax(logits, axis=-1)
        return jnp.einsum('bhqk,bkhd->bqhd', p.astype(v.dtype), v)

    qn_b = q_nope.reshape(b, nb, BLOCK, MLA_HEADS, QK_NOPE).swapaxes(0, 1)
    qr_b = q_rope.reshape(b, nb, BLOCK, MLA_HEADS, QK_ROPE).swapaxes(0, 1)
    o = lax.map(block, (qn_b, qr_b))
    return o.swapaxes(0, 1).reshape(b, s, MLA_WIDTH)


def expert_choice_ffn(h, w_router, w_gate, w_up, w_down):
    n, d = h.shape
    cap = CAPACITY_FACTOR * n // N_EXPERTS
    aff = jax.nn.softmax((h @ w_router).astype(jnp.float32), axis=-1)
    gate, idx = lax.top_k(aff.T, cap)
    xe = h[idx]
    a = jnp.einsum('ecd,edf->ecf', xe, w_gate)
    u = jnp.einsum('ecd,edf->ecf', xe, w_up)
    ye = jnp.einsum('ecf,efd->ecd', jax.nn.silu(a) * u, w_down) * gate[..., None].astype(h.dtype)
    return jnp.zeros_like(h).at[idx.reshape(-1)].add(ye.reshape(-1, d))


def encoder_layer(x, cos, sin, slopes, g_attn, w_in, sink, g_cq, w_uq, g_ckv, w_uk, w_uv,
                  g_ya, g_yb, w_out, g_ffn, w_router, w_gate, w_up, w_down):
    b, s, d = x.shape
    h = rmsnorm(x, g_attn)
    proj = h @ w_in
    offs = [int(o) for o in np.cumsum(SPLIT_SIZES)[:-1]]
    q_a, k_a, v_a, c_q, c_kv, k_r = jnp.split(proj, offs, axis=-1)
    y_a = window_gqa(q_a.reshape(b, s, SWA_HEADS, HEAD_DIM),
                     k_a.reshape(b, s, SWA_KV_HEADS, HEAD_DIM),
                     v_a.reshape(b, s, SWA_KV_HEADS, HEAD_DIM), sink, slopes)
    y_b = mla(c_q, c_kv, k_r, g_cq, w_uq, g_ckv, w_uk, w_uv, cos, sin)
    mixed = jnp.concatenate([rmsnorm(y_a, g_ya), rmsnorm(y_b, g_yb)], axis=-1) @ w_out
    x = x + mixed
    h2 = rmsnorm(x, g_ffn).reshape(b * s, d)
    x = x + expert_choice_ffn(h2, w_router, w_gate, w_up, w_down).reshape(b, s, d)
    return x


def trunk(x, g_attn, w_in, sink, g_cq, w_uq, g_ckv, w_uk, w_uv, g_ya, g_yb, w_out,
          g_ffn, w_router, w_gate, w_up, w_down, g_final):
    cos, sin = rope_tables(x.shape[1])
    slopes = alibi_slopes(SWA_HEADS)
    for l in range(DEPTH):
        x = encoder_layer(x, cos, sin, slopes, g_attn[l], w_in[l], sink[l], g_cq[l], w_uq[l],
                          g_ckv[l], w_uk[l], w_uv[l], g_ya[l], g_yb[l], w_out[l], g_ffn[l],
                          w_router[l], w_gate[l], w_up[l], w_down[l])
    return rmsnorm(x, g_final)


def setup_inputs(seed: int = 0) -> dict:
    key = jax.random.key(seed)
    ks = jax.random.split(key, 20)
    f32 = jnp.float32

    def w(k, shape, fan_in):
        return jax.random.normal(k, shape, f32) * (fan_in ** -0.5)

    def gain(k, shape):
        return 1.0 + 0.02 * jax.random.normal(k, shape, f32)

    return {
        'x_prompt': jax.random.normal(ks[0], (BATCH, SEQ, D_MODEL), f32),
        'x_sample': jax.random.normal(ks[1], (DEC_BATCH, DEC_SEQ, D_MODEL), f32),
        'g_attn': gain(ks[2], (DEPTH, D_MODEL)),
        'w_in': w(ks[3], (DEPTH, D_MODEL, IN_WIDTH), D_MODEL),
        'sink': 0.5 * jax.random.normal(ks[4], (DEPTH, SWA_HEADS), f32),
        'g_cq': gain(ks[5], (DEPTH, Q_LORA)),
        'w_uq': w(ks[6], (DEPTH, Q_LORA, MLA_HEADS * (QK_NOPE + QK_ROPE)), Q_LORA),
        'g_ckv': gain(ks[7], (DEPTH, KV_LORA)),
        'w_uk': w(ks[8], (DEPTH, KV_LORA, MLA_HEADS * QK_NOPE), KV_LORA),
        'w_uv': w(ks[9], (DEPTH, KV_LORA, MLA_HEADS * V_DIM), KV_LORA),
        'g_ya': gain(ks[10], (DEPTH, SWA_WIDTH)),
        'g_yb': gain(ks[11], (DEPTH, MLA_WIDTH)),
        'w_out': w(ks[12], (DEPTH, MIX_WIDTH, D_MODEL), MIX_WIDTH),
        'g_ffn': gain(ks[13], (DEPTH, D_MODEL)),
        'w_router': w(ks[14], (DEPTH, D_MODEL, N_EXPERTS), D_MODEL),
        'w_gate': w(ks[15], (DEPTH, N_EXPERTS, D_MODEL, D_EXPERT), D_MODEL),
        'w_up': w(ks[16], (DEPTH, N_EXPERTS, D_MODEL, D_EXPERT), D_MODEL),
        'w_down': w(ks[17], (DEPTH, N_EXPERTS, D_EXPERT, D_MODEL), D_EXPERT),
        'g_final': gain(ks[18], (D_MODEL,)),
    }


def reference(x_prompt, x_sample, g_attn, w_in, sink, g_cq, w_uq, g_ckv, w_uk, w_uv,
              g_ya, g_yb, w_out, g_ffn, w_router, w_gate, w_up, w_down, g_final):
    y_prompt = trunk(x_prompt, g_attn, w_in, sink, g_cq, w_uq, g_ckv, w_uk, w_uv, g_ya, g_yb,
                     w_out, g_ffn, w_router, w_gate, w_up, w_down, g_final)
    y_sample = trunk(x_sample, g_attn, w_in, sink, g_cq, w_uq, g_ckv, w_uk, w_uv, g_ya, g_yb,
                     w_out, g_ffn, w_router, w_gate, w_up, w_down, g_final)
    return (y_prompt, y_sample)
```

```python
import functools

import numpy as np
import jax
import jax.numpy as jnp
from jax import lax
from jax.experimental import pallas as pl
from jax.experimental.pallas import tpu as pltpu

F32 = jnp.float32
BF16 = jnp.bfloat16

D_MODEL = 2048
HEAD_DIM = 128
SWA_HEADS = 8
SWA_KV_HEADS = 2
SWA_GROUP = SWA_HEADS // SWA_KV_HEADS
WINDOW = 128
BLOCK = 128
MLA_HEADS = 8
QK_NOPE = 128
QK_ROPE = 64
V_DIM = 128
Q_LORA = 512
KV_LORA = 512
ROPE_THETA = 10000.0
SWA_WIDTH = SWA_HEADS * HEAD_DIM
SWA_KV_WIDTH = SWA_KV_HEADS * HEAD_DIM
MLA_WIDTH = MLA_HEADS * V_DIM
N_EXPERTS = 16
D_EXPERT = 1408
CAPACITY_FACTOR = 2
EPS = 1e-6

OFF_QA = 0
OFF_KA = OFF_QA + SWA_WIDTH
OFF_VA = OFF_KA + SWA_KV_WIDTH
OFF_CQ = OFF_VA + SWA_KV_WIDTH
OFF_CKV = OFF_CQ + Q_LORA
OFF_KR = OFF_CKV + KV_LORA
OFF_KRR = OFF_KR + 128
IN_EXT = OFF_KRR + 128
MLA_QK = 256

SWA_SCALE = HEAD_DIM ** -0.5
MLA_SCALE = (QK_NOPE + QK_ROPE) ** -0.5
NEG_BIG = -1e30

V7X_VMEM_BYTES = 64 * 1024 * 1024
VMEM_LIMIT = V7X_VMEM_BYTES - 8 * 1024 * 1024


def _cparams(*sem):
    return pltpu.CompilerParams(dimension_semantics=sem, vmem_limit_bytes=VMEM_LIMIT)


def _const_spec(shape):
    nd = len(shape)
    return pl.BlockSpec(shape, lambda *_: (0,) * nd, pipeline_mode=pl.Buffered(1))


def _rms(x, g):
    return x * lax.rsqrt(jnp.mean(x * x, axis=-1, keepdims=True) + EPS) * g


def _dot(a, b):
    return jnp.dot(a, b, preferred_element_type=F32)


def _dot_nt(a, b):
    return lax.dot_general(a, b, (((1,), (1,)), ((), ())), preferred_element_type=F32)


def _in_proj_kernel(x_ref, g_ref, win_ref, gcq_ref, wqm_ref, wqr_ref, gckv_ref, wuk_ref, wuvt_ref,
                    cos_ref, sin_ref, qa_ref, ka_ref, va_ref, qm_ref, km_ref, vt_ref):
    h = _rms(x_ref[...], g_ref[...]).astype(BF16)
    proj = _dot(h, win_ref[...])
    qa_ref[...] = (proj[:, OFF_QA:OFF_KA] * SWA_SCALE).astype(BF16)
    ka_ref[...] = proj[:, OFF_KA:OFF_VA].astype(BF16)
    va_ref[...] = proj[:, OFF_VA:OFF_CQ].astype(BF16)
    cos = cos_ref[...]
    sin = sin_ref[...]
    krope = (proj[:, OFF_KR:OFF_KRR] * cos + proj[:, OFF_KRR:IN_EXT] * sin).astype(BF16)
    cqn = _rms(proj[:, OFF_CQ:OFF_CKV], gcq_ref[...]).astype(BF16)
    qmain = _dot(cqn, wqm_ref[...])
    qrot = _dot(cqn, wqr_ref[...])
    for hh in range(MLA_HEADS):
        lo = hh * MLA_QK
        qm_ref[:, lo:lo + QK_NOPE] = (qmain[:, lo:lo + QK_NOPE] * MLA_SCALE).astype(BF16)
        rp = qmain[:, lo + QK_NOPE:lo + MLA_QK] * cos + qrot[:, hh * 128:(hh + 1) * 128] * sin
        qm_ref[:, lo + QK_NOPE:lo + MLA_QK] = (rp * MLA_SCALE).astype(BF16)
    ckvn = _rms(proj[:, OFF_CKV:OFF_KR], gckv_ref[...]).astype(BF16)
    kn = _dot(ckvn, wuk_ref[...])
    for hh in range(MLA_HEADS):
        lo = hh * MLA_QK
        km_ref[:, lo:lo + QK_NOPE] = kn[:, hh * QK_NOPE:(hh + 1) * QK_NOPE].astype(BF16)
        km_ref[:, lo + QK_NOPE:lo + MLA_QK] = krope
    vt_ref[...] = _dot_nt(wuvt_ref[...], ckvn).astype(BF16)


def _in_proj(x, lw, cos128, sin128, tm):
    b, s, d = x.shape
    row = lambda w: pl.BlockSpec((None, tm, w), lambda i, j: (i, j, 0))
    tab = pl.BlockSpec((tm, 128), lambda i, j: (j, 0))
    out_shapes = (
        jax.ShapeDtypeStruct((b, s, SWA_WIDTH), BF16),
        jax.ShapeDtypeStruct((b, s, SWA_KV_WIDTH), BF16),
        jax.ShapeDtypeStruct((b, s, SWA_KV_WIDTH), BF16),
        jax.ShapeDtypeStruct((b, s, MLA_HEADS * MLA_QK), BF16),
        jax.ShapeDtypeStruct((b, s, MLA_HEADS * MLA_QK), BF16),
        jax.ShapeDtypeStruct((b, MLA_WIDTH, s), BF16),
    )
    return pl.pallas_call(
        _in_proj_kernel,
        grid=(b, s // tm),
        in_specs=[
            row(d),
            _const_spec((1, d)),
            _const_spec((d, IN_EXT)),
            _const_spec((1, Q_LORA)),
            _const_spec((Q_LORA, MLA_HEADS * MLA_QK)),
            _const_spec((Q_LORA, MLA_HEADS * 128)),
            _const_spec((1, KV_LORA)),
            _const_spec((KV_LORA, MLA_HEADS * QK_NOPE)),
            _const_spec((MLA_WIDTH, KV_LORA)),
            tab, tab,
        ],
        out_specs=(row(SWA_WIDTH), row(SWA_KV_WIDTH), row(SWA_KV_WIDTH),
                   row(MLA_HEADS * MLA_QK), row(MLA_HEADS * MLA_QK),
                   pl.BlockSpec((None, MLA_WIDTH, tm), lambda i, j: (i, 0, j))),
        out_shape=out_shapes,
        compiler_params=_cparams("parallel", "parallel"),
    )(x, lw["g_attn"], lw["w_in"], lw["g_cq"], lw["w_uq_main"], lw["w_uq_rot"], lw["g_ckv"],
      lw["w_uk"], lw["w_uvt"], cos128, sin128)


def _alibi_slopes():
    return [float(np.float32(2.0 ** (-8.0 * (i + 1) / SWA_HEADS))) for i in range(SWA_HEADS)]


def _swa_kernel(sink_ref, q_ref, kp_ref, kc_ref, kn_ref, vp_ref, vc_ref, vn_ref, o_ref, *, seq):
    n = pl.program_id(1)
    qi = lax.broadcasted_iota(jnp.int32, (BLOCK, 3 * BLOCK), 0)
    kj = lax.broadcasted_iota(jnp.int32, (BLOCK, 3 * BLOCK), 1)
    rel = kj - BLOCK - qi
    spos = n * BLOCK + kj - BLOCK
    valid = (jnp.abs(rel) <= WINDOW) & (spos >= 0) & (spos < seq)
    dist = jnp.abs(rel).astype(F32)
    slopes = _alibi_slopes()
    kall = jnp.concatenate([kp_ref[...], kc_ref[...], kn_ref[...]], axis=0)
    vall = jnp.concatenate([vp_ref[...], vc_ref[...], vn_ref[...]], axis=0)
    for kvh in range(SWA_KV_HEADS):
        kh = kall[:, kvh * HEAD_DIM:(kvh + 1) * HEAD_DIM]
        vh = vall[:, kvh * HEAD_DIM:(kvh + 1) * HEAD_DIM]
        for g in range(SWA_GROUP):
            hd = kvh * SWA_GROUP + g
            q = q_ref[:, hd * HEAD_DIM:(hd + 1) * HEAD_DIM]
            logits = _dot_nt(q, kh)
            logits = jnp.where(valid, logits - slopes[hd] * dist, NEG_BIG)
            sk = sink_ref[hd]
            m = jnp.maximum(jnp.max(logits, axis=-1, keepdims=True), sk)
            p = jnp.exp(logits - m)
            denom = jnp.sum(p, axis=-1, keepdims=True) + jnp.exp(sk - m)
            o = _dot(p.astype(BF16), vh) / denom
            o_ref[:, hd * HEAD_DIM:(hd + 1) * HEAD_DIM] = o.astype(BF16)


def _swa(qa, ka, va, sink):
    b, s, _ = qa.shape
    nb = s // BLOCK
    kv = lambda f: pl.BlockSpec((None, BLOCK, SWA_KV_WIDTH), f)
    prev = lambda i, j: (i, jnp.maximum(j - 1, 0), 0)
    cur = lambda i, j: (i, j, 0)
    nxt = lambda i, j: (i, jnp.minimum(j + 1, nb - 1), 0)
    return pl.pallas_call(
        functools.partial(_swa_kernel, seq=s),
        grid=(b, nb),
        in_specs=[pl.BlockSpec(memory_space=pltpu.SMEM),
                  pl.BlockSpec((None, BLOCK, SWA_WIDTH), cur),
                  kv(prev), kv(cur), kv(nxt), kv(prev), kv(cur), kv(nxt)],
        out_specs=pl.BlockSpec((None, BLOCK, SWA_WIDTH), cur),
        out_shape=jax.ShapeDtypeStruct((b, s, SWA_WIDTH), BF16),
        compiler_params=_cparams("parallel", "parallel"),
    )(sink, qa, ka, ka, ka, va, va, va)


def _mla_kernel(q_ref, k_ref, vt_ref, o_ref, m_ref, l_ref, acc_ref):
    j = pl.program_id(3)

    @pl.when(j == 0)
    def _():
        m_ref[...] = jnp.full_like(m_ref, NEG_BIG)
        l_ref[...] = jnp.zeros_like(l_ref)
        acc_ref[...] = jnp.zeros_like(acc_ref)

    st = _dot_nt(k_ref[...], q_ref[...])
    m_old = m_ref[...]
    m_new = jnp.maximum(m_old, jnp.max(st, axis=0, keepdims=True))
    alpha = jnp.exp(m_old - m_new)
    p = jnp.exp(st - m_new)
    l_ref[...] = alpha * l_ref[...] + jnp.sum(p, axis=0, keepdims=True)
    acc_ref[...] = alpha * acc_ref[...] + _dot(vt_ref[...], p.astype(BF16))
    m_ref[...] = m_new

    @pl.when(j == pl.num_programs(3) - 1)
    def _():
        o_ref[...] = (acc_ref[...] / l_ref[...]).T.astype(BF16)


def _mla(qm, km, vt, tq, tk):
    b, s, _ = qm.shape
    return pl.pallas_call(
        _mla_kernel,
        grid=(b, MLA_HEADS, s // tq, s // tk),
        in_specs=[pl.BlockSpec((None, tq, MLA_QK), lambda bb, h, i, j: (bb, i, h)),
                  pl.BlockSpec((None, tk, MLA_QK), lambda bb, h, i, j: (bb, j, h)),
                  pl.BlockSpec((None, V_DIM, tk), lambda bb, h, i, j: (bb, h, j))],
        out_specs=pl.BlockSpec((None, tq, V_DIM), lambda bb, h, i, j: (bb, i, h)),
        out_shape=jax.ShapeDtypeStruct((b, s, MLA_WIDTH), BF16),
        scratch_shapes=[pltpu.VMEM((1, tq), F32), pltpu.VMEM((1, tq), F32),
                        pltpu.VMEM((V_DIM, tq), F32)],
        compiler_params=_cparams("parallel", "parallel", "parallel", "arbitrary"),
    )(qm, km, vt)


def _out_proj_kernel(x_ref, ya_ref, yb_ref, gya_ref, gyb_ref, woa_ref, wob_ref, gffn_ref, wrt_ref,
                     xo_ref, h2_ref, afft_ref):
    yan = _rms(ya_ref[...].astype(F32), gya_ref[...]).astype(BF16)
    ybn = _rms(yb_ref[...].astype(F32), gyb_ref[...]).astype(BF16)
    xn = x_ref[...] + _dot(yan, woa_ref[...]) + _dot(ybn, wob_ref[...])
    xo_ref[...] = xn
    h2 = _rms(xn, gffn_ref[...]).astype(BF16)
    h2_ref[...] = h2
    logits = _dot_nt(wrt_ref[...], h2)
    m = jnp.max(logits, axis=0, keepdims=True)
    p = jnp.exp(logits - m)
    afft_ref[...] = p / jnp.sum(p, axis=0, keepdims=True)


def _out_proj(x2, ya2, yb2, lw, tm):
    n, d = x2.shape
    row = lambda w: pl.BlockSpec((tm, w), lambda i: (i, 0))
    return pl.pallas_call(
        _out_proj_kernel,
        grid=(n // tm,),
        in_specs=[row(d), row(SWA_WIDTH), row(MLA_WIDTH),
                  _const_spec((1, SWA_WIDTH)), _const_spec((1, MLA_WIDTH)),
                  _const_spec((SWA_WIDTH, d)), _const_spec((MLA_WIDTH, d)),
                  _const_spec((1, d)), _const_spec((N_EXPERTS, d))],
        out_specs=(row(d), row(d), pl.BlockSpec((N_EXPERTS, tm), lambda i: (0, i))),
        out_shape=(jax.ShapeDtypeStruct((n, d), F32), jax.ShapeDtypeStruct((n, d), BF16),
                   jax.ShapeDtypeStruct((N_EXPERTS, n), F32)),
        compiler_params=_cparams("parallel"),
    )(x2, ya2, yb2, lw["g_ya"], lw["g_yb"], lw["w_out_a"], lw["w_out_b"], lw["g_ffn"], lw["w_router_t"])


def _ffn_kernel(xe_ref, wgu_ref, wd_ref, gate_ref, ye_ref):
    au = _dot(xe_ref[...], wgu_ref[...])
    a = au[:, :D_EXPERT]
    u = au[:, D_EXPERT:]
    act = (a / (1.0 + jnp.exp(-a)) * u).astype(BF16)
    ye_ref[...] = _dot(act, wd_ref[...]) * gate_ref[...]


def _ffn(xe, wgu, wd, gate, tt):
    e, c, d = xe.shape
    return pl.pallas_call(
        _ffn_kernel,
        grid=(e, c // tt),
        in_specs=[pl.BlockSpec((None, tt, d), lambda i, j: (i, j, 0)),
                  pl.BlockSpec((None, d, 2 * D_EXPERT), lambda i, j: (i, 0, 0)),
                  pl.BlockSpec((None, D_EXPERT, d), lambda i, j: (i, 0, 0)),
                  pl.BlockSpec((None, tt, 1), lambda i, j: (i, j, 0))],
        out_specs=pl.BlockSpec((None, tt, d), lambda i, j: (i, j, 0)),
        out_shape=jax.ShapeDtypeStruct((e, c, d), F32),
        compiler_params=_cparams("parallel", "arbitrary"),
    )(xe, wgu, wd, gate)


def _final_norm_kernel(x_ref, g_ref, o_ref):
    o_ref[...] = _rms(x_ref[...], g_ref[...])


def _final_norm(x2, g, tm):
    n, d = x2.shape
    return pl.pallas_call(
        _final_norm_kernel,
        grid=(n // tm,),
        in_specs=[pl.BlockSpec((tm, d), lambda i: (i, 0)), _const_spec((1, d))],
        out_specs=pl.BlockSpec((tm, d), lambda i: (i, 0)),
        out_shape=jax.ShapeDtypeStruct((n, d), F32),
        compiler_params=_cparams("parallel"),
    )(x2, g)


def _rot_half_cols(w):
    half = w.shape[-1] // 2
    return jnp.concatenate([-w[..., half:], w[..., :half]], axis=-1)


def _prep_layer(l, g_attn, w_in, sink, g_cq, w_uq, g_ckv, w_uk, w_uv, g_ya, g_yb, w_out, g_ffn,
                w_router, w_gate, w_up, w_down):
    wi = w_in[l]
    w_kr = wi[:, OFF_KR:OFF_KR + QK_ROPE]
    zpad = jnp.zeros((D_MODEL, 128 - QK_ROPE), F32)
    w_in_ext = jnp.concatenate([wi[:, :OFF_KR], w_kr, zpad, _rot_half_cols(w_kr), zpad], axis=1)
    wq = w_uq[l].reshape(Q_LORA, MLA_HEADS, QK_NOPE + QK_ROPE)
    wq_rope = wq[:, :, QK_NOPE:]
    zq = jnp.zeros((Q_LORA, MLA_HEADS, 128 - QK_ROPE), F32)
    w_uq_main = jnp.concatenate([wq, zq], axis=-1).reshape(Q_LORA, MLA_HEADS * MLA_QK)
    w_uq_rot = jnp.concatenate([_rot_half_cols(wq_rope), zq], axis=-1).reshape(Q_LORA, MLA_HEADS * 128)
    return dict(
        g_attn=g_attn[l][None], w_in=w_in_ext.astype(BF16), sink=sink[l],
        g_cq=g_cq[l][None], w_uq_main=w_uq_main.astype(BF16), w_uq_rot=w_uq_rot.astype(BF16),
        g_ckv=g_ckv[l][None], w_uk=w_uk[l].astype(BF16), w_uvt=w_uv[l].T.astype(BF16),
        g_ya=g_ya[l][None], g_yb=g_yb[l][None],
        w_out_a=w_out[l][:SWA_WIDTH].astype(BF16), w_out_b=w_out[l][SWA_WIDTH:].astype(BF16),
        g_ffn=g_ffn[l][None], w_router_t=w_router[l].T.astype(BF16),
        w_gu=jnp.concatenate([w_gate[l], w_up[l]], axis=-1).astype(BF16),
        w_down=w_down[l].astype(BF16),
    )


def _rope_tables(s):
    inv = 1.0 / (ROPE_THETA ** (jnp.arange(0, QK_ROPE, 2, dtype=F32) / QK_ROPE))
    ang = jnp.arange(s, dtype=F32)[:, None] * inv[None, :]
    z = jnp.zeros((s, 128 - QK_ROPE), F32)
    cos, sin = jnp.cos(ang), jnp.sin(ang)
    return jnp.concatenate([cos, cos, z], axis=1), jnp.concatenate([sin, sin, z], axis=1)


def _pick(n, pref):
    t = pref
    while n % t:
        t //= 2
    return t


def _layer(x, lw, cos128, sin128):
    b, s, d = x.shape
    n = b * s
    qa, ka, va, qm, km, vt = _in_proj(x, lw, cos128, sin128, _pick(s, 512))
    ya = _swa(qa, ka, va, lw["sink"])
    yb = _mla(qm, km, vt, _pick(s, 512), _pick(s, 512))
    x2, h2, afft = _out_proj(x.reshape(n, d), ya.reshape(n, SWA_WIDTH), yb.reshape(n, MLA_WIDTH), lw,
                             _pick(n, 512))
    cap = CAPACITY_FACTOR * n // N_EXPERTS
    gate, idx = lax.top_k(afft, cap)
    xe = h2[idx]
    ye = _ffn(xe, lw["w_gu"], lw["w_down"], gate[..., None], _pick(cap, 256))
    x2 = x2 + jnp.zeros_like(x2).at[idx.reshape(-1)].add(ye.reshape(-1, d))
    return x2.reshape(b, s, d)


def _trunk(x, layers, g_final):
    b, s, d = x.shape
    cos128, sin128 = _rope_tables(s)
    for lw in layers:
        x = _layer(x, lw, cos128, sin128)
    return _final_norm(x.reshape(b * s, d), g_final[None], _pick(b * s, 512)).reshape(b, s, d)


def kernel(x_prompt, x_sample, g_attn, w_in, sink, g_cq, w_uq, g_ckv, w_uk, w_uv, g_ya, g_yb, w_out,
           g_ffn, w_router, w_gate, w_up, w_down, g_final):
    depth = w_in.shape[0]
    layers = [_prep_layer(l, g_attn, w_in, sink, g_cq, w_uq, g_ckv, w_uk, w_uv, g_ya, g_yb, w_out,
                          g_ffn, w_router, w_gate, w_up, w_down) for l in range(depth)]
    return _trunk(x_prompt, layers, g_final), _trunk(x_sample, layers, g_final)
```

```python
import functools

import numpy as np
import jax
import jax.numpy as jnp
from jax import lax
from jax.experimental import pallas as pl
from jax.experimental.pallas import tpu as pltpu

F32 = jnp.float32
BF16 = jnp.bfloat16

D_MODEL = 2048
HEAD_DIM = 128
SWA_HEADS = 8
SWA_KV_HEADS = 2
SWA_GROUP = SWA_HEADS // SWA_KV_HEADS
WINDOW = 128
BLOCK = 128
MLA_HEADS = 8
QK_NOPE = 128
QK_ROPE = 64
V_DIM = 128
Q_LORA = 512
KV_LORA = 512
ROPE_THETA = 10000.0
SWA_WIDTH = SWA_HEADS * HEAD_DIM
SWA_KV_WIDTH = SWA_KV_HEADS * HEAD_DIM
MLA_WIDTH = MLA_HEADS * V_DIM
N_EXPERTS = 16
D_EXPERT = 1408
CAPACITY_FACTOR = 2
EPS = 1e-6

OFF_QA = 0
OFF_KA = OFF_QA + SWA_WIDTH
OFF_VA = OFF_KA + SWA_KV_WIDTH
OFF_CQ = OFF_VA + SWA_KV_WIDTH
OFF_CKV = OFF_CQ + Q_LORA
OFF_KR = OFF_CKV + KV_LORA
OFF_KRR = OFF_KR + 128
IN_EXT = OFF_KRR + 128
MLA_QK = 256

SWA_SCALE = HEAD_DIM ** -0.5
MLA_SCALE = (QK_NOPE + QK_ROPE) ** -0.5 * float(np.log2(np.e))
NEG_BIG = -1e30
MLA_TQ = 512
MLA_TK = 512

V7X_VMEM_BYTES = 64 * 1024 * 1024
VMEM_LIMIT = V7X_VMEM_BYTES - 8 * 1024 * 1024


def _cparams(*sem):
    return pltpu.CompilerParams(dimension_semantics=sem, vmem_limit_bytes=VMEM_LIMIT)


def _const_spec(shape):
    nd = len(shape)
    return pl.BlockSpec(shape, lambda *_: (0,) * nd, pipeline_mode=pl.Buffered(1))


def _rms(x, g):
    return x * lax.rsqrt(jnp.mean(x * x, axis=-1, keepdims=True) + EPS) * g


def _dot(a, b):
    return jnp.dot(a, b, preferred_element_type=F32)


def _dot_nt(a, b):
    return lax.dot_general(a, b, (((1,), (1,)), ((), ())), preferred_element_type=F32)


def _in_proj_kernel(x_ref, g_ref, win_ref, gcq_ref, wqm_ref, wqr_ref, gckv_ref, wuk_ref, wuvt_ref,
                    cos_ref, sin_ref, qa_ref, ka_ref, va_ref, qm_ref, km_ref, vt_ref):
    h = _rms(x_ref[...], g_ref[...]).astype(BF16)
    proj = _dot(h, win_ref[...])
    qa_ref[...] = (proj[:, OFF_QA:OFF_KA] * SWA_SCALE).astype(BF16)
    ka_ref[...] = proj[:, OFF_KA:OFF_VA].astype(BF16)
    va_ref[...] = proj[:, OFF_VA:OFF_CQ].astype(BF16)
    cos = cos_ref[...]
    sin = sin_ref[...]
    krope = (proj[:, OFF_KR:OFF_KRR] * cos + proj[:, OFF_KRR:IN_EXT] * sin).astype(BF16)
    cqn = _rms(proj[:, OFF_CQ:OFF_CKV], gcq_ref[...]).astype(BF16)
    qmain = _dot(cqn, wqm_ref[...])
    qrot = _dot(cqn, wqr_ref[...])
    for hh in range(MLA_HEADS):
        lo = hh * MLA_QK
        qm_ref[:, lo:lo + QK_NOPE] = (qmain[:, lo:lo + QK_NOPE] * MLA_SCALE).astype(BF16)
        rp = qmain[:, lo + QK_NOPE:lo + MLA_QK] * cos + qrot[:, hh * 128:(hh + 1) * 128] * sin
        qm_ref[:, lo + QK_NOPE:lo + MLA_QK] = (rp * MLA_SCALE).astype(BF16)
    ckvn = _rms(proj[:, OFF_CKV:OFF_KR], gckv_ref[...]).astype(BF16)
    kn = _dot(ckvn, wuk_ref[...])
    for hh in range(MLA_HEADS):
        lo = hh * MLA_QK
        km_ref[:, lo:lo + QK_NOPE] = kn[:, hh * QK_NOPE:(hh + 1) * QK_NOPE].astype(BF16)
        km_ref[:, lo + QK_NOPE:lo + MLA_QK] = krope
    vt_ref[...] = _dot_nt(wuvt_ref[...], ckvn).astype(BF16)


def _in_proj(x, lw, cos128, sin128, tm):
    b, s, d = x.shape
    row = lambda w: pl.BlockSpec((None, tm, w), lambda i, j: (i, j, 0))
    tab = pl.BlockSpec((tm, 128), lambda i, j: (j, 0))
    out_shapes = (
        jax.ShapeDtypeStruct((b, s, SWA_WIDTH), BF16),
        jax.ShapeDtypeStruct((b, s, SWA_KV_WIDTH), BF16),
        jax.ShapeDtypeStruct((b, s, SWA_KV_WIDTH), BF16),
        jax.ShapeDtypeStruct((b, s, MLA_HEADS * MLA_QK), BF16),
        jax.ShapeDtypeStruct((b, s, MLA_HEADS * MLA_QK), BF16),
        jax.ShapeDtypeStruct((b, s // tm, MLA_WIDTH, tm), BF16),
    )
    return pl.pallas_call(
        _in_proj_kernel,
        grid=(b, s // tm),
        in_specs=[
            row(d),
            _const_spec((1, d)),
            _const_spec((d, IN_EXT)),
            _const_spec((1, Q_LORA)),
            _const_spec((Q_LORA, MLA_HEADS * MLA_QK)),
            _const_spec((Q_LORA, MLA_HEADS * 128)),
            _const_spec((1, KV_LORA)),
            _const_spec((KV_LORA, MLA_HEADS * QK_NOPE)),
            _const_spec((MLA_WIDTH, KV_LORA)),
            tab, tab,
        ],
        out_specs=(row(SWA_WIDTH), row(SWA_KV_WIDTH), row(SWA_KV_WIDTH),
                   row(MLA_HEADS * MLA_QK), row(MLA_HEADS * MLA_QK),
                   pl.BlockSpec((None, None, MLA_WIDTH, tm), lambda i, j: (i, j, 0, 0))),
        out_shape=out_shapes,
        compiler_params=_cparams("parallel", "parallel"),
    )(x, lw["g_attn"], lw["w_in"], lw["g_cq"], lw["w_uq_main"], lw["w_uq_rot"], lw["g_ckv"],
      lw["w_uk"], lw["w_uvt"], cos128, sin128)


def _alibi_slopes():
    return [float(np.float32(2.0 ** (-8.0 * (i + 1) / SWA_HEADS))) for i in range(SWA_HEADS)]


def _swa_kernel(sink_ref, q_ref, kp_ref, kc_ref, kn_ref, vp_ref, vc_ref, vn_ref, o_ref, *, seq):
    n = pl.program_id(1)
    qi = lax.broadcasted_iota(jnp.int32, (BLOCK, 3 * BLOCK), 0)
    kj = lax.broadcasted_iota(jnp.int32, (BLOCK, 3 * BLOCK), 1)
    rel = kj - BLOCK - qi
    spos = n * BLOCK + kj - BLOCK
    valid = (jnp.abs(rel) <= WINDOW) & (spos >= 0) & (spos < seq)
    dist = jnp.abs(rel).astype(F32)
    slopes = _alibi_slopes()
    kall = jnp.concatenate([kp_ref[...], kc_ref[...], kn_ref[...]], axis=0)
    vall = jnp.concatenate([vp_ref[...], vc_ref[...], vn_ref[...]], axis=0)
    for kvh in range(SWA_KV_HEADS):
        kh = kall[:, kvh * HEAD_DIM:(kvh + 1) * HEAD_DIM]
        vh = vall[:, kvh * HEAD_DIM:(kvh + 1) * HEAD_DIM]
        for g in range(SWA_GROUP):
            hd = kvh * SWA_GROUP + g
            q = q_ref[:, hd * HEAD_DIM:(hd + 1) * HEAD_DIM]
            logits = _dot_nt(q, kh)
            logits = jnp.where(valid, logits - slopes[hd] * dist, NEG_BIG)
            sk = sink_ref[hd]
            m = jnp.maximum(jnp.max(logits, axis=-1, keepdims=True), sk)
            p = jnp.exp(logits - m)
            denom = jnp.sum(p, axis=-1, keepdims=True) + jnp.exp(sk - m)
            o = _dot(p.astype(BF16), vh) / denom
            o_ref[:, hd * HEAD_DIM:(hd + 1) * HEAD_DIM] = o.astype(BF16)


def _swa(qa, ka, va, sink):
    b, s, _ = qa.shape
    nb = s // BLOCK
    kv = lambda f: pl.BlockSpec((None, BLOCK, SWA_KV_WIDTH), f)
    prev = lambda i, j: (i, jnp.maximum(j - 1, 0), 0)
    cur = lambda i, j: (i, j, 0)
    nxt = lambda i, j: (i, jnp.minimum(j + 1, nb - 1), 0)
    return pl.pallas_call(
        functools.partial(_swa_kernel, seq=s),
        grid=(b, nb),
        in_specs=[pl.BlockSpec(memory_space=pltpu.SMEM),
                  pl.BlockSpec((None, BLOCK, SWA_WIDTH), cur),
                  kv(prev), kv(cur), kv(nxt), kv(prev), kv(cur), kv(nxt)],
        out_specs=pl.BlockSpec((None, BLOCK, SWA_WIDTH), cur),
        out_shape=jax.ShapeDtypeStruct((b, s, SWA_WIDTH), BF16),
        compiler_params=_cparams("parallel", "parallel"),
    )(sink, qa, ka, ka, ka, va, va, va)


MLA_NSUB = 2


def _mla_kernel(q_ref, k_ref, vt_ref, o_ref, st_ref):
    nsub, _, tk, tq = st_ref.shape
    nchunk = vt_ref.shape[0]
    assert nchunk % 2 == 0
    qs = [q_ref[t * tq:(t + 1) * tq, :] for t in range(nsub)]

    def produce(t, slot, c):
        k0 = pl.multiple_of(c * tk, tk)
        st = _dot_nt(k_ref[pl.ds(k0, tk), :], qs[t])
        st_ref[t, slot] = st
        return jnp.max(st, axis=0, keepdims=True)

    def consume(t, slot, c, mx, state):
        m, l, acc = state
        m_new = jnp.maximum(m, mx)
        alpha = jnp.exp2(m - m_new)
        p = jnp.exp2(st_ref[t, slot] - m_new)
        l = alpha * l + jnp.sum(p, axis=0, keepdims=True)
        acc = alpha * acc + _dot(vt_ref[c], p.astype(BF16))
        return m_new, l, acc

    def body(i, carry):
        mx0, states = carry
        c = 2 * i
        mx1 = [produce(t, 1, c + 1) for t in range(nsub)]
        states = [consume(t, 0, c, mx0[t], states[t]) for t in range(nsub)]
        c2 = jnp.minimum(c + 2, nchunk - 1)
        mx0 = [produce(t, 0, c2) for t in range(nsub)]
        states = [consume(t, 1, c + 1, mx1[t], states[t]) for t in range(nsub)]
        return mx0, states

    state0 = (jnp.full((1, tq), NEG_BIG, F32), jnp.zeros((1, tq), F32), jnp.zeros((V_DIM, tq), F32))
    init = ([produce(t, 0, 0) for t in range(nsub)], [state0] * nsub)
    _, states = lax.fori_loop(0, nchunk // 2, body, init, unroll=2)
    for t in range(nsub):
        _, l, acc = states[t]
        o_ref[t * tq:(t + 1) * tq, :] = (acc / l).T.astype(BF16)


def _mla(qm, km, vt):
    b, s, _ = qm.shape
    nchunk, tk = vt.shape[1], vt.shape[3]
    tq = _pick(s, MLA_TQ)
    nsub = MLA_NSUB if s % (MLA_NSUB * tq) == 0 else 1
    return pl.pallas_call(
        _mla_kernel,
        grid=(b, MLA_HEADS, s // (nsub * tq)),
        in_specs=[pl.BlockSpec((None, nsub * tq, MLA_QK), lambda bb, h, i: (bb, i, h)),
                  pl.BlockSpec((None, s, MLA_QK), lambda bb, h, i: (bb, 0, h)),
                  pl.BlockSpec((None, nchunk, V_DIM, tk), lambda bb, h, i: (bb, 0, h, 0))],
        out_specs=pl.BlockSpec((None, nsub * tq, V_DIM), lambda bb, h, i: (bb, i, h)),
        out_shape=jax.ShapeDtypeStruct((b, s, MLA_WIDTH), BF16),
        scratch_shapes=[pltpu.VMEM((nsub, 2, tk, tq), F32)],
        compiler_params=_cparams("parallel", "parallel", "arbitrary"),
    )(qm, km, vt)


def _out_proj_kernel(x_ref, ya_ref, yb_ref, gya_ref, gyb_ref, woa_ref, wob_ref, gffn_ref, wrt_ref,
                     xo_ref, h2_ref, afft_ref):
    yan = _rms(ya_ref[...].astype(F32), gya_ref[...]).astype(BF16)
    ybn = _rms(yb_ref[...].astype(F32), gyb_ref[...]).astype(BF16)
    xn = x_ref[...] + _dot(yan, woa_ref[...]) + _dot(ybn, wob_ref[...])
    xo_ref[...] = xn
    h2 = _rms(xn, gffn_ref[...]).astype(BF16)
    h2_ref[...] = h2
    logits = _dot_nt(wrt_ref[...], h2)
    m = jnp.max(logits, axis=0, keepdims=True)
    p = jnp.exp(logits - m)
    afft_ref[...] = p / jnp.sum(p, axis=0, keepdims=True)


def _out_proj(x2, ya2, yb2, lw, tm):
    n, d = x2.shape
    row = lambda w: pl.BlockSpec((tm, w), lambda i: (i, 0))
    return pl.pallas_call(
        _out_proj_kernel,
        grid=(n // tm,),
        in_specs=[row(d), row(SWA_WIDTH), row(MLA_WIDTH),
                  _const_spec((1, SWA_WIDTH)), _const_spec((1, MLA_WIDTH)),
                  _const_spec((SWA_WIDTH, d)), _const_spec((MLA_WIDTH, d)),
                  _const_spec((1, d)), _const_spec((N_EXPERTS, d))],
        out_specs=(row(d), row(d), pl.BlockSpec((N_EXPERTS, tm), lambda i: (0, i))),
        out_shape=(jax.ShapeDtypeStruct((n, d), F32), jax.ShapeDtypeStruct((n, d), BF16),
                   jax.ShapeDtypeStruct((N_EXPERTS, n), F32)),
        compiler_params=_cparams("parallel"),
    )(x2, ya2, yb2, lw["g_ya"], lw["g_yb"], lw["w_out_a"], lw["w_out_b"], lw["g_ffn"], lw["w_router_t"])


def _ffn_kernel(xe_ref, wgu_ref, wd_ref, gate_ref, ye_ref):
    au = _dot(xe_ref[...], wgu_ref[...])
    a = au[:, :D_EXPERT]
    u = au[:, D_EXPERT:]
    act = (a / (1.0 + jnp.exp(-a)) * u).astype(BF16)
    ye_ref[...] = _dot(act, wd_ref[...]) * gate_ref[...]


def _ffn(xe, wgu, wd, gate, tt):
    e, c, d = xe.shape
    return pl.pallas_call(
        _ffn_kernel,
        grid=(e, c // tt),
        in_specs=[pl.BlockSpec((None, tt, d), lambda i, j: (i, j, 0)),
                  pl.BlockSpec((None, d, 2 * D_EXPERT), lambda i, j: (i, 0, 0)),
                  pl.BlockSpec((None, D_EXPERT, d), lambda i, j: (i, 0, 0)),
                  pl.BlockSpec((None, tt, 1), lambda i, j: (i, j, 0))],
        out_specs=pl.BlockSpec((None, tt, d), lambda i, j: (i, j, 0)),
        out_shape=jax.ShapeDtypeStruct((e, c, d), F32),
        compiler_params=_cparams("parallel", "arbitrary"),
    )(xe, wgu, wd, gate)


def _final_norm_kernel(x_ref, g_ref, o_ref):
    o_ref[...] = _rms(x_ref[...], g_ref[...])


def _final_norm(x2, g, tm):
    n, d = x2.shape
    return pl.pallas_call(
        _final_norm_kernel,
        grid=(n // tm,),
        in_specs=[pl.BlockSpec((tm, d), lambda i: (i, 0)), _const_spec((1, d))],
        out_specs=pl.BlockSpec((tm, d), lambda i: (i, 0)),
        out_shape=jax.ShapeDtypeStruct((n, d), F32),
        compiler_params=_cparams("parallel"),
    )(x2, g)


def _rot_half_cols(w):
    half = w.shape[-1] // 2
    return jnp.concatenate([-w[..., half:], w[..., :half]], axis=-1)


def _prep_layer(l, g_attn, w_in, sink, g_cq, w_uq, g_ckv, w_uk, w_uv, g_ya, g_yb, w_out, g_ffn,
                w_router, w_gate, w_up, w_down):
    wi = w_in[l]
    w_kr = wi[:, OFF_KR:OFF_KR + QK_ROPE]
    zpad = jnp.zeros((D_MODEL, 128 - QK_ROPE), F32)
    w_in_ext = jnp.concatenate([wi[:, :OFF_KR], w_kr, zpad, _rot_half_cols(w_kr), zpad], axis=1)
    wq = w_uq[l].reshape(Q_LORA, MLA_HEADS, QK_NOPE + QK_ROPE)
    wq_rope = wq[:, :, QK_NOPE:]
    zq = jnp.zeros((Q_LORA, MLA_HEADS, 128 - QK_ROPE), F32)
    w_uq_main = jnp.concatenate([wq, zq], axis=-1).reshape(Q_LORA, MLA_HEADS * MLA_QK)
    w_uq_rot = jnp.concatenate([_rot_half_cols(wq_rope), zq], axis=-1).reshape(Q_LORA, MLA_HEADS * 128)
    return dict(
        g_attn=g_attn[l][None], w_in=w_in_ext.astype(BF16), sink=sink[l],
        g_cq=g_cq[l][None], w_uq_main=w_uq_main.astype(BF16), w_uq_rot=w_uq_rot.astype(BF16),
        g_ckv=g_ckv[l][None], w_uk=w_uk[l].astype(BF16), w_uvt=w_uv[l].T.astype(BF16),
        g_ya=g_ya[l][None], g_yb=g_yb[l][None],
        w_out_a=w_out[l][:SWA_WIDTH].astype(BF16), w_out_b=w_out[l][SWA_WIDTH:].astype(BF16),
        g_ffn=g_ffn[l][None], w_router_t=w_router[l].T.astype(BF16),
        w_gu=jnp.concatenate([w_gate[l], w_up[l]], axis=-1).astype(BF16),
        w_down=w_down[l].astype(BF16),
    )


def _rope_tables(s):
    inv = 1.0 / (ROPE_THETA ** (jnp.arange(0, QK_ROPE, 2, dtype=F32) / QK_ROPE))
    ang = jnp.arange(s, dtype=F32)[:, None] * inv[None, :]
    z = jnp.zeros((s, 128 - QK_ROPE), F32)
    cos, sin = jnp.cos(ang), jnp.sin(ang)
    return jnp.concatenate([cos, cos, z], axis=1), jnp.concatenate([sin, sin, z], axis=1)


def _pick(n, pref):
    t = pref
    while n % t:
        t //= 2
    return t


def _layer(x, lw, cos128, sin128):
    b, s, d = x.shape
    n = b * s
    qa, ka, va, qm, km, vt = _in_proj(x, lw, cos128, sin128, _pick(s, MLA_TK))
    ya = _swa(qa, ka, va, lw["sink"])
    yb = _mla(qm, km, vt)
    x2, h2, afft = _out_proj(x.reshape(n, d), ya.reshape(n, SWA_WIDTH), yb.reshape(n, MLA_WIDTH), lw,
                             _pick(n, 512))
    cap = CAPACITY_FACTOR * n // N_EXPERTS
    gate, idx = lax.top_k(afft, cap)
    xe = h2[idx]
    ye = _ffn(xe, lw["w_gu"], lw["w_down"], gate[..., None], _pick(cap, 256))
    x2 = x2 + jnp.zeros_like(x2).at[idx.reshape(-1)].add(ye.reshape(-1, d))
    return x2.reshape(b, s, d)


def _trunk(x, layers, g_final):
    b, s, d = x.shape
    cos128, sin128 = _rope_tables(s)
    for lw in layers:
        x = _layer(x, lw, cos128, sin128)
    return _final_norm(x.reshape(b * s, d), g_final[None], _pick(b * s, 512)).reshape(b, s, d)


def kernel(x_prompt, x_sample, g_attn, w_in, sink, g_cq, w_uq, g_ckv, w_uk, w_uv, g_ya, g_yb, w_out,
           g_ffn, w_router, w_gate, w_up, w_down, g_final):
    depth = w_in.shape[0]
    layers = [_prep_layer(l, g_attn, w_in, sink, g_cq, w_uq, g_ckv, w_uk, w_uv, g_ya, g_yb, w_out,
                          g_ffn, w_router, w_gate, w_up, w_down) for l in range(depth)]
    return _trunk(x_prompt, layers, g_final), _trunk(x_sample, layers, g_final)
```

```python
import functools

import numpy as np
import jax
import jax.numpy as jnp
from jax import lax
from jax.experimental import pallas as pl
from jax.experimental.pallas import tpu as pltpu

F32 = jnp.float32
BF16 = jnp.bfloat16

D_MODEL = 2048
HEAD_DIM = 128
SWA_HEADS = 8
SWA_KV_HEADS = 2
SWA_GROUP = SWA_HEADS // SWA_KV_HEADS
WINDOW = 128
BLOCK = 128
MLA_HEADS = 8
QK_NOPE = 128
QK_ROPE = 64
V_DIM = 128
Q_LORA = 512
KV_LORA = 512
ROPE_THETA = 10000.0
SWA_WIDTH = SWA_HEADS * HEAD_DIM
SWA_KV_WIDTH = SWA_KV_HEADS * HEAD_DIM
MLA_WIDTH = MLA_HEADS * V_DIM
N_EXPERTS = 16
D_EXPERT = 1408
CAPACITY_FACTOR = 2
EPS = 1e-6

OFF_QA = 0
OFF_KA = OFF_QA + SWA_WIDTH
OFF_VA = OFF_KA + SWA_KV_WIDTH
OFF_CQ = OFF_VA + SWA_KV_WIDTH
OFF_CKV = OFF_CQ + Q_LORA
OFF_KR = OFF_CKV + KV_LORA
OFF_KRR = OFF_KR + 128
IN_EXT = OFF_KRR + 128
MLA_QK = 256

SWA_SCALE = HEAD_DIM ** -0.5
MLA_SCALE = (QK_NOPE + QK_ROPE) ** -0.5 * float(np.log2(np.e))
NEG_BIG = -1e30
MLA_TQ = 512
MLA_TK = 512

V7X_VMEM_BYTES = 64 * 1024 * 1024
VMEM_LIMIT = V7X_VMEM_BYTES - 8 * 1024 * 1024


def _cparams(*sem):
    return pltpu.CompilerParams(dimension_semantics=sem, vmem_limit_bytes=VMEM_LIMIT)


def _const_spec(shape):
    nd = len(shape)
    return pl.BlockSpec(shape, lambda *_: (0,) * nd, pipeline_mode=pl.Buffered(1))


def _rms(x, g):
    return x * lax.rsqrt(jnp.mean(x * x, axis=-1, keepdims=True) + EPS) * g


def _dot(a, b):
    return jnp.dot(a, b, preferred_element_type=F32)


def _dot_nt(a, b):
    return lax.dot_general(a, b, (((1,), (1,)), ((), ())), preferred_element_type=F32)


def _in_proj_kernel(x_ref, g_ref, win_ref, gcq_ref, wqm_ref, wqr_ref, gckv_ref, wuk_ref, wuvt_ref,
                    cos_ref, sin_ref, qa_ref, ka_ref, va_ref, qm_ref, km_ref, vt_ref):
    h = _rms(x_ref[...], g_ref[...]).astype(BF16)
    proj = _dot(h, win_ref[...])
    qa_ref[...] = (proj[:, OFF_QA:OFF_KA] * SWA_SCALE).astype(BF16)
    ka_ref[...] = proj[:, OFF_KA:OFF_VA].astype(BF16)
    va_ref[...] = proj[:, OFF_VA:OFF_CQ].astype(BF16)
    cos = cos_ref[...]
    sin = sin_ref[...]
    krope = (proj[:, OFF_KR:OFF_KRR] * cos + proj[:, OFF_KRR:IN_EXT] * sin).astype(BF16)
    cqn = _rms(proj[:, OFF_CQ:OFF_CKV], gcq_ref[...]).astype(BF16)
    qmain = _dot(cqn, wqm_ref[...])
    qrot = _dot(cqn, wqr_ref[...])
    for hh in range(MLA_HEADS):
        lo = hh * MLA_QK
        qm_ref[:, lo:lo + QK_NOPE] = (qmain[:, lo:lo + QK_NOPE] * MLA_SCALE).astype(BF16)
        rp = qmain[:, lo + QK_NOPE:lo + MLA_QK] * cos + qrot[:, hh * 128:(hh + 1) * 128] * sin
        qm_ref[:, lo + QK_NOPE:lo + MLA_QK] = (rp * MLA_SCALE).astype(BF16)
    ckvn = _rms(proj[:, OFF_CKV:OFF_KR], gckv_ref[...]).astype(BF16)
    kn = _dot(ckvn, wuk_ref[...])
    for hh in range(MLA_HEADS):
        lo = hh * MLA_QK
        km_ref[:, lo:lo + QK_NOPE] = kn[:, hh * QK_NOPE:(hh + 1) * QK_NOPE].astype(BF16)
        km_ref[:, lo + QK_NOPE:lo + MLA_QK] = krope
    vt_ref[...] = _dot_nt(wuvt_ref[...], ckvn).astype(BF16)


def _in_proj(x, lw, cos128, sin128, tm):
    b, s, d = x.shape
    row = lambda w: pl.BlockSpec((None, tm, w), lambda i, j: (i, j, 0))
    tab = pl.BlockSpec((tm, 128), lambda i, j: (j, 0))
    out_shapes = (
        jax.ShapeDtypeStruct((b, s, SWA_WIDTH), BF16),
        jax.ShapeDtypeStruct((b, s, SWA_KV_WIDTH), BF16),
        jax.ShapeDtypeStruct((b, s, SWA_KV_WIDTH), BF16),
        jax.ShapeDtypeStruct((b, s, MLA_HEADS * MLA_QK), BF16),
        jax.ShapeDtypeStruct((b, s, MLA_HEADS * MLA_QK), BF16),
        jax.ShapeDtypeStruct((b, s // tm, MLA_WIDTH, tm), BF16),
    )
    return pl.pallas_call(
        _in_proj_kernel,
        grid=(b, s // tm),
        in_specs=[
            row(d),
            _const_spec((1, d)),
            _const_spec((d, IN_EXT)),
            _const_spec((1, Q_LORA)),
            _const_spec((Q_LORA, MLA_HEADS * MLA_QK)),
            _const_spec((Q_LORA, MLA_HEADS * 128)),
            _const_spec((1, KV_LORA)),
            _const_spec((KV_LORA, MLA_HEADS * QK_NOPE)),
            _const_spec((MLA_WIDTH, KV_LORA)),
            tab, tab,
        ],
        out_specs=(row(SWA_WIDTH), row(SWA_KV_WIDTH), row(SWA_KV_WIDTH),
                   row(MLA_HEADS * MLA_QK), row(MLA_HEADS * MLA_QK),
                   pl.BlockSpec((None, None, MLA_WIDTH, tm), lambda i, j: (i, j, 0, 0))),
        out_shape=out_shapes,
        compiler_params=_cparams("parallel", "parallel"),
    )(x, lw["g_attn"], lw["w_in"], lw["g_cq"], lw["w_uq_main"], lw["w_uq_rot"], lw["g_ckv"],
      lw["w_uk"], lw["w_uvt"], cos128, sin128)


def _alibi_slopes():
    return [float(np.float32(2.0 ** (-8.0 * (i + 1) / SWA_HEADS))) for i in range(SWA_HEADS)]


def _swa_kernel(sink_ref, q_ref, kp_ref, kc_ref, kn_ref, vp_ref, vc_ref, vn_ref, o_ref, *, seq):
    n = pl.program_id(1)
    qi = lax.broadcasted_iota(jnp.int32, (BLOCK, 3 * BLOCK), 0)
    kj = lax.broadcasted_iota(jnp.int32, (BLOCK, 3 * BLOCK), 1)
    rel = kj - BLOCK - qi
    spos = n * BLOCK + kj - BLOCK
    valid = (jnp.abs(rel) <= WINDOW) & (spos >= 0) & (spos < seq)
    dist = jnp.abs(rel).astype(F32)
    slopes = _alibi_slopes()
    kall = jnp.concatenate([kp_ref[...], kc_ref[...], kn_ref[...]], axis=0)
    vall = jnp.concatenate([vp_ref[...], vc_ref[...], vn_ref[...]], axis=0)
    for kvh in range(SWA_KV_HEADS):
        kh = kall[:, kvh * HEAD_DIM:(kvh + 1) * HEAD_DIM]
        vh = vall[:, kvh * HEAD_DIM:(kvh + 1) * HEAD_DIM]
        for g in range(SWA_GROUP):
            hd = kvh * SWA_GROUP + g
            q = q_ref[:, hd * HEAD_DIM:(hd + 1) * HEAD_DIM]
            logits = _dot_nt(q, kh)
            logits = jnp.where(valid, logits - slopes[hd] * dist, NEG_BIG)
            sk = sink_ref[hd]
            m = jnp.maximum(jnp.max(logits, axis=-1, keepdims=True), sk)
            p = jnp.exp(logits - m)
            denom = jnp.sum(p, axis=-1, keepdims=True) + jnp.exp(sk - m)
            o = _dot(p.astype(BF16), vh) / denom
            o_ref[:, hd * HEAD_DIM:(hd + 1) * HEAD_DIM] = o.astype(BF16)


def _swa(qa, ka, va, sink):
    b, s, _ = qa.shape
    nb = s // BLOCK
    kv = lambda f: pl.BlockSpec((None, BLOCK, SWA_KV_WIDTH), f)
    prev = lambda i, j: (i, jnp.maximum(j - 1, 0), 0)
    cur = lambda i, j: (i, j, 0)
    nxt = lambda i, j: (i, jnp.minimum(j + 1, nb - 1), 0)
    return pl.pallas_call(
        functools.partial(_swa_kernel, seq=s),
        grid=(b, nb),
        in_specs=[pl.BlockSpec(memory_space=pltpu.SMEM),
                  pl.BlockSpec((None, BLOCK, SWA_WIDTH), cur),
                  kv(prev), kv(cur), kv(nxt), kv(prev), kv(cur), kv(nxt)],
        out_specs=pl.BlockSpec((None, BLOCK, SWA_WIDTH), cur),
        out_shape=jax.ShapeDtypeStruct((b, s, SWA_WIDTH), BF16),
        compiler_params=_cparams("parallel", "parallel"),
    )(sink, qa, ka, ka, ka, va, va, va)


MLA_NSUB = 2


def _mla_kernel(q_ref, k_ref, vt_ref, o_ref, st_ref):
    nsub, _, tk, tq = st_ref.shape
    nchunk = vt_ref.shape[0]
    assert nchunk % 2 == 0
    qs = [q_ref[t * tq:(t + 1) * tq, :] for t in range(nsub)]

    def produce(t, slot, c):
        k0 = pl.multiple_of(c * tk, tk)
        st = _dot_nt(k_ref[pl.ds(k0, tk), :], qs[t])
        st_ref[t, slot] = st
        return jnp.max(st, axis=0, keepdims=True)

    def consume(t, slot, c, mx, state):
        m, l, acc = state
        m_new = jnp.maximum(m, mx)
        alpha = jnp.exp2(m - m_new)
        p = jnp.exp2(st_ref[t, slot] - m_new)
        l = alpha * l + jnp.sum(p, axis=0, keepdims=True)
        acc = alpha * acc + _dot(vt_ref[c], p.astype(BF16))
        return m_new, l, acc

    def body(i, carry):
        mx0, states = carry
        c = 2 * i
        mx1 = [produce(t, 1, c + 1) for t in range(nsub)]
        states = [consume(t, 0, c, mx0[t], states[t]) for t in range(nsub)]
        c2 = jnp.minimum(c + 2, nchunk - 1)
        mx0 = [produce(t, 0, c2) for t in range(nsub)]
        states = [consume(t, 1, c + 1, mx1[t], states[t]) for t in range(nsub)]
        return mx0, states

    state0 = (jnp.full((1, tq), NEG_BIG, F32), jnp.zeros((1, tq), F32), jnp.zeros((V_DIM, tq), F32))
    init = ([produce(t, 0, 0) for t in range(nsub)], [state0] * nsub)
    _, states = lax.fori_loop(0, nchunk // 2, body, init, unroll=2)
    for t in range(nsub):
        _, l, acc = states[t]
        o_ref[t * tq:(t + 1) * tq, :] = (acc / l).T.astype(BF16)


def _mla(qm, km, vt):
    b, s, _ = qm.shape
    nchunk, tk = vt.shape[1], vt.shape[3]
    tq = _pick(s, MLA_TQ)
    nsub = MLA_NSUB if s % (MLA_NSUB * tq) == 0 else 1
    return pl.pallas_call(
        _mla_kernel,
        grid=(b, MLA_HEADS, s // (nsub * tq)),
        in_specs=[pl.BlockSpec((None, nsub * tq, MLA_QK), lambda bb, h, i: (bb, i, h)),
                  pl.BlockSpec((None, s, MLA_QK), lambda bb, h, i: (bb, 0, h)),
                  pl.BlockSpec((None, nchunk, V_DIM, tk), lambda bb, h, i: (bb, 0, h, 0))],
        out_specs=pl.BlockSpec((None, nsub * tq, V_DIM), lambda bb, h, i: (bb, i, h)),
        out_shape=jax.ShapeDtypeStruct((b, s, MLA_WIDTH), BF16),
        scratch_shapes=[pltpu.VMEM((nsub, 2, tk, tq), F32)],
        compiler_params=_cparams("parallel", "parallel", "arbitrary"),
    )(qm, km, vt)


def _out_proj_kernel(x_ref, ya_ref, yb_ref, gya_ref, gyb_ref, woa_ref, wob_ref, gffn_ref, wrt_ref,
                     xo_ref, h2_ref, afft_ref):
    yan = _rms(ya_ref[...].astype(F32), gya_ref[...]).astype(BF16)
    ybn = _rms(yb_ref[...].astype(F32), gyb_ref[...]).astype(BF16)
    xn = x_ref[...] + _dot(yan, woa_ref[...]) + _dot(ybn, wob_ref[...])
    xo_ref[...] = xn
    h2 = _rms(xn, gffn_ref[...])
    h2_ref[...] = h2
    logits = _dot_nt(wrt_ref[...], h2.astype(BF16))
    m = jnp.max(logits, axis=0, keepdims=True)
    p = jnp.exp(logits - m)
    afft_ref[...] = p / jnp.sum(p, axis=0, keepdims=True)


def _out_proj(x2, ya2, yb2, lw, tm):
    n, d = x2.shape
    row = lambda w: pl.BlockSpec((tm, w), lambda i: (i, 0))
    return pl.pallas_call(
        _out_proj_kernel,
        grid=(n // tm,),
        in_specs=[row(d), row(SWA_WIDTH), row(MLA_WIDTH),
                  _const_spec((1, SWA_WIDTH)), _const_spec((1, MLA_WIDTH)),
                  _const_spec((SWA_WIDTH, d)), _const_spec((MLA_WIDTH, d)),
                  _const_spec((1, d)), _const_spec((N_EXPERTS, d))],
        out_specs=(row(d), row(d), pl.BlockSpec((N_EXPERTS, tm), lambda i: (0, i))),
        out_shape=(jax.ShapeDtypeStruct((n, d), F32), jax.ShapeDtypeStruct((n, d), F32),
                   jax.ShapeDtypeStruct((N_EXPERTS, n), F32)),
        compiler_params=_cparams("parallel"),
    )(x2, ya2, yb2, lw["g_ya"], lw["g_yb"], lw["w_out_a"], lw["w_out_b"], lw["g_ffn"], lw["w_router_t"])


LANES = 128
SEL_CJ = 1024


def _token_cumsum(mask, upper, ones, lstrict):
    mb = mask.astype(BF16)
    within = _dot(mb, upper)
    tot = _dot(mb, ones)
    rowpre = _dot(lstrict, tot.astype(BF16))
    return within, tot, rowpre


def _select_kernel(a_ref, idx_ref, gate_ref, pos_ref, rowpre_ref, *, cap):
    a = a_ref[...]
    g = a.shape[0]
    bits = pltpu.bitcast(a, jnp.int32)

    def radix_step(i, t):
        cand = t | jnp.left_shift(jnp.int32(1), 30 - i)
        cnt = jnp.sum(jnp.where(bits >= cand, 1.0, 0.0), keepdims=True)
        return jnp.where(cnt >= cap, cand, t)

    thr = lax.fori_loop(0, 31, radix_step, jnp.zeros((1, 1), jnp.int32))
    ii = lax.broadcasted_iota(jnp.int32, (LANES, LANES), 0)
    jj = lax.broadcasted_iota(jnp.int32, (LANES, LANES), 1)
    upper = jnp.where(ii <= jj, 1.0, 0.0).astype(BF16)
    ones = jnp.ones((LANES, LANES), BF16)
    gi = lax.broadcasted_iota(jnp.int32, (g, g), 0)
    gj = lax.broadcasted_iota(jnp.int32, (g, g), 1)
    lstrict = jnp.where(gj < gi, 1.0, 0.0).astype(BF16)

    gt = bits > thr
    eq = bits == thr
    need = cap - jnp.sum(jnp.where(gt, 1.0, 0.0), keepdims=True)
    eqf = jnp.where(eq, 1.0, 0.0)
    w_eq, _, rp_eq = _token_cumsum(eqf, upper, ones, lstrict)
    tie_rank = w_eq - eqf + rp_eq
    self_ = jnp.where(gt, 1.0, jnp.where(tie_rank < need, eqf, 0.0))
    w, tot, rp = _token_cumsum(self_, upper, ones, lstrict)
    pos_ref[...] = jnp.where(self_ > 0.0, w - 1.0 + rp, -1.0)
    rpt = rp.T
    rowpre_ref[...] = rpt[0:1, :]

    rc = rp + tot
    wt = w.T.astype(BF16)
    pre_hi = jnp.floor(rpt[0:16] * (1.0 / 64.0))
    pre_lo = rpt[0:16] - 64.0 * pre_hi
    pre_hi = pre_hi.astype(BF16)
    pre_lo = pre_lo.astype(BF16)
    at = a.T
    a_h = at.astype(BF16)
    r1 = at - a_h.astype(F32)
    a_m = r1.astype(BF16)
    a_l = (r1 - a_m.astype(F32)).astype(BF16)
    cj_n = min(SEL_CJ, cap)
    for jc in range(cap // cj_n):
        j = (lax.broadcasted_iota(jnp.int32, (1, cj_n), 1) + jc * cj_n).astype(F32)
        rct = jnp.tile(rc, (1, cj_n // LANES))
        row = jnp.sum(jnp.where(rct <= j, 1.0, 0.0), axis=0, keepdims=True)
        grow = lax.broadcasted_iota(jnp.int32, (g, cj_n), 0).astype(F32)
        onehot = jnp.where(grow == row, 1.0, 0.0).astype(BF16)
        wg = _dot(wt, onehot)
        pre = 64.0 * _dot(pre_hi, onehot)[0:1] + _dot(pre_lo, onehot)[0:1]
        jloc = j - pre
        lane = jnp.sum(jnp.where(wg <= jloc, 1.0, 0.0), axis=0, keepdims=True)
        idx_ref[:, jc * cj_n:(jc + 1) * cj_n] = (row * LANES + lane).astype(jnp.int32)
        ag = _dot(a_h, onehot) + _dot(a_m, onehot) + _dot(a_l, onehot)
        lrow = lax.broadcasted_iota(jnp.int32, (LANES, cj_n), 0).astype(F32)
        gate_ref[:, jc * cj_n:(jc + 1) * cj_n] = jnp.sum(jnp.where(lrow == lane, ag, 0.0), axis=0,
                                                          keepdims=True)


def _select(afft, cap):
    e, n = afft.shape
    g = n // LANES
    blk = lambda w: pl.BlockSpec((None, 1, w), lambda i: (i, 0, 0))
    tok = pl.BlockSpec((None, g, LANES), lambda i: (i, 0, 0))
    return pl.pallas_call(
        functools.partial(_select_kernel, cap=cap),
        grid=(e,),
        in_specs=[tok],
        out_specs=(blk(cap), blk(cap), tok, blk(g)),
        out_shape=(jax.ShapeDtypeStruct((e, 1, cap), jnp.int32), jax.ShapeDtypeStruct((e, 1, cap), F32),
                   jax.ShapeDtypeStruct((e, g, LANES), F32), jax.ShapeDtypeStruct((e, 1, g), F32)),
        compiler_params=_cparams("parallel"),
    )(afft.reshape(e, g, LANES))


FFN_TT = 256


def _ffn_kernel(idx_ref, h2_hbm, wgu_ref, wd_ref, gate_ref, ye_ref, xbuf, sem):
    tt = xbuf.shape[1]
    nj = pl.num_programs(1)
    step = pl.program_id(0) * nj + pl.program_id(1)
    last = pl.num_programs(0) * nj - 1
    slot = step % 2

    def row_copy(tok, sl, r):
        return pltpu.make_async_copy(h2_hbm.at[pl.ds(tok, 1), :], xbuf.at[sl, pl.ds(r, 1), :], sem.at[sl])

    def issue(s, sl):
        for r in range(tt):
            row_copy(idx_ref[s * tt + r], sl, r).start()

    def wait_slot(sl):
        pltpu.make_async_copy(h2_hbm.at[pl.ds(0, tt), :], xbuf.at[sl], sem.at[sl]).wait()

    @pl.when(step == 0)
    def _():
        issue(0, 0)

    wait_slot(slot)
    x = xbuf[slot].astype(BF16)
    issue(jnp.minimum(step + 1, last), 1 - slot)
    au = _dot(x, wgu_ref[...])
    a = au[:, :D_EXPERT]
    u = au[:, D_EXPERT:]
    act = (a / (1.0 + jnp.exp(-a)) * u).astype(BF16)
    ye_ref[...] = (_dot(act, wd_ref[...]) * gate_ref[...]).astype(BF16)

    @pl.when(step == last)
    def _():
        wait_slot(1 - slot)


def _ffn(idx_flat, h2, wgu, wd, gate, cap):
    e = wgu.shape[0]
    d = h2.shape[1]
    tt = _pick(cap, FFN_TT)
    return pl.pallas_call(
        _ffn_kernel,
        grid_spec=pltpu.PrefetchScalarGridSpec(
            num_scalar_prefetch=1,
            grid=(e, cap // tt),
            in_specs=[pl.BlockSpec(memory_space=pl.ANY),
                      pl.BlockSpec((None, d, 2 * D_EXPERT), lambda i, j, idx: (i, 0, 0)),
                      pl.BlockSpec((None, D_EXPERT, d), lambda i, j, idx: (i, 0, 0)),
                      pl.BlockSpec((None, tt, 1), lambda i, j, idx: (i, j, 0))],
            out_specs=pl.BlockSpec((None, tt, d), lambda i, j, idx: (i, j, 0)),
            scratch_shapes=[pltpu.VMEM((2, tt, d), F32), pltpu.SemaphoreType.DMA((2,))]),
        out_shape=jax.ShapeDtypeStruct((e, cap, d), BF16),
        compiler_params=_cparams("arbitrary", "arbitrary"),
    )(idx_flat, h2, wgu, wd, gate)


CMB_TT = 128
CMB_BLK = 16
CMB_KC = 256
CMB_ROWS = N_EXPERTS * CMB_TT + N_EXPERTS * 2 * CMB_BLK + CMB_KC


def _combine_kernel(rowpre_ref, x_ref, pos_ref, ye_hbm, o_ref, buf, sem, acc_ref, *, g):
    i = pl.program_id(0)
    last = pl.num_programs(0) - 1
    slot = i % 2

    def plan(tile):
        starts, nblks, offs = [], [], []
        off = jnp.int32(0)
        for e in range(N_EXPERTS):
            lo = rowpre_ref[e * (g + 1) + tile]
            hi = rowpre_ref[e * (g + 1) + tile + 1]
            start = (lo >> 4) << 4
            nblk = jnp.where(hi > lo, (hi - start + (CMB_BLK - 1)) >> 4, 0)
            starts.append(start)
            nblks.append(nblk)
            offs.append(off)
            off = off + nblk
        return starts, nblks, offs, off

    def block_copy(e, src_row, dst_row, sl):
        return pltpu.make_async_copy(ye_hbm.at[e, pl.ds(src_row, CMB_BLK), :],
                                     buf.at[sl, pl.ds(dst_row, CMB_BLK), :], sem.at[sl])

    def issue(tile, sl):
        starts, nblks, offs, _ = plan(tile)
        for e in range(N_EXPERTS):
            def body(b, c, e=e):
                block_copy(e, pl.multiple_of(starts[e] + b * CMB_BLK, CMB_BLK),
                           pl.multiple_of((offs[e] + b) * CMB_BLK, CMB_BLK), sl).start()
                return c
            lax.fori_loop(0, nblks[e], body, 0)

    def wait(tile, sl):
        total = plan(tile)[3]

        def body(b, c):
            block_copy(0, 0, 0, sl).wait()
            return c
        lax.fori_loop(0, total, body, 0)

    @pl.when(i == 0)
    def _():
        buf[...] = jnp.zeros_like(buf)
        issue(0, 0)

    @pl.when(i < last)
    def _():
        issue(i + 1, 1 - slot)

    wait(i, slot)
    starts, nblks, offs, total = plan(i)
    pos = pos_ref[...]
    cols = []
    for e in range(N_EXPERTS):
        pe = pos[:, e:e + 1]
        shift = (offs[e] * CMB_BLK - starts[e]).astype(F32)
        col = jnp.where(pe >= 0.0, pe + shift, -1.0)
        cols.append(jnp.broadcast_to(col, (CMB_TT, CMB_KC)))
    acc_ref[...] = x_ref[...]

    def chunk(k, c):
        k0 = pl.multiple_of(k * CMB_KC, CMB_KC)
        r = (lax.broadcasted_iota(jnp.int32, (CMB_TT, CMB_KC), 1) + k0).astype(F32)
        hit = jnp.where(cols[0] == r, 1.0, 0.0)
        for e in range(1, N_EXPERTS):
            hit = hit + jnp.where(cols[e] == r, 1.0, 0.0)
        acc_ref[...] += _dot(hit.astype(BF16), buf[slot, pl.ds(k0, CMB_KC), :])
        return c

    lax.fori_loop(0, (total * CMB_BLK + (CMB_KC - 1)) // CMB_KC, chunk, 0)
    o_ref[...] = acc_ref[...]


def _combine(rowpre_flat, x2, pos_t, ye, g):
    n, d = x2.shape
    assert n == g * CMB_TT
    return pl.pallas_call(
        functools.partial(_combine_kernel, g=g),
        grid_spec=pltpu.PrefetchScalarGridSpec(
            num_scalar_prefetch=1,
            grid=(g,),
            in_specs=[pl.BlockSpec((CMB_TT, d), lambda i, rp: (i, 0)),
                      pl.BlockSpec((CMB_TT, N_EXPERTS), lambda i, rp: (i, 0)),
                      pl.BlockSpec(memory_space=pl.ANY)],
            out_specs=pl.BlockSpec((CMB_TT, d), lambda i, rp: (i, 0)),
            scratch_shapes=[pltpu.VMEM((2, CMB_ROWS, d), BF16), pltpu.SemaphoreType.DMA((2,)),
                            pltpu.VMEM((CMB_TT, d), F32)]),
        out_shape=jax.ShapeDtypeStruct((n, d), F32),
        compiler_params=_cparams("arbitrary"),
    )(rowpre_flat, x2, pos_t, ye)


def _final_norm_kernel(x_ref, g_ref, o_ref):
    o_ref[...] = _rms(x_ref[...], g_ref[...])


def _final_norm(x2, g, tm):
    n, d = x2.shape
    return pl.pallas_call(
        _final_norm_kernel,
        grid=(n // tm,),
        in_specs=[pl.BlockSpec((tm, d), lambda i: (i, 0)), _const_spec((1, d))],
        out_specs=pl.BlockSpec((tm, d), lambda i: (i, 0)),
        out_shape=jax.ShapeDtypeStruct((n, d), F32),
        compiler_params=_cparams("parallel"),
    )(x2, g)


def _rot_half_cols(w):
    half = w.shape[-1] // 2
    return jnp.concatenate([-w[..., half:], w[..., :half]], axis=-1)


def _prep_layer(l, g_attn, w_in, sink, g_cq, w_uq, g_ckv, w_uk, w_uv, g_ya, g_yb, w_out, g_ffn,
                w_router, w_gate, w_up, w_down):
    wi = w_in[l]
    w_kr = wi[:, OFF_KR:OFF_KR + QK_ROPE]
    zpad = jnp.zeros((D_MODEL, 128 - QK_ROPE), F32)
    w_in_ext = jnp.concatenate([wi[:, :OFF_KR], w_kr, zpad, _rot_half_cols(w_kr), zpad], axis=1)
    wq = w_uq[l].reshape(Q_LORA, MLA_HEADS, QK_NOPE + QK_ROPE)
    wq_rope = wq[:, :, QK_NOPE:]
    zq = jnp.zeros((Q_LORA, MLA_HEADS, 128 - QK_ROPE), F32)
    w_uq_main = jnp.concatenate([wq, zq], axis=-1).reshape(Q_LORA, MLA_HEADS * MLA_QK)
    w_uq_rot = jnp.concatenate([_rot_half_cols(wq_rope), zq], axis=-1).reshape(Q_LORA, MLA_HEADS * 128)
    return dict(
        g_attn=g_attn[l][None], w_in=w_in_ext.astype(BF16), sink=sink[l],
        g_cq=g_cq[l][None], w_uq_main=w_uq_main.astype(BF16), w_uq_rot=w_uq_rot.astype(BF16),
        g_ckv=g_ckv[l][None], w_uk=w_uk[l].astype(BF16), w_uvt=w_uv[l].T.astype(BF16),
        g_ya=g_ya[l][None], g_yb=g_yb[l][None],
        w_out_a=w_out[l][:SWA_WIDTH].astype(BF16), w_out_b=w_out[l][SWA_WIDTH:].astype(BF16),
        g_ffn=g_ffn[l][None], w_router_t=w_router[l].T.astype(BF16),
        w_gu=jnp.concatenate([w_gate[l], w_up[l]], axis=-1).astype(BF16),
        w_down=w_down[l].astype(BF16),
    )


def _rope_tables(s):
    inv = 1.0 / (ROPE_THETA ** (jnp.arange(0, QK_ROPE, 2, dtype=F32) / QK_ROPE))
    ang = jnp.arange(s, dtype=F32)[:, None] * inv[None, :]
    z = jnp.zeros((s, 128 - QK_ROPE), F32)
    cos, sin = jnp.cos(ang), jnp.sin(ang)
    return jnp.concatenate([cos, cos, z], axis=1), jnp.concatenate([sin, sin, z], axis=1)


def _pick(n, pref):
    t = pref
    while n % t:
        t //= 2
    return t


def _layer(x, lw, cos128, sin128):
    b, s, d = x.shape
    n = b * s
    qa, ka, va, qm, km, vt = _in_proj(x, lw, cos128, sin128, _pick(s, MLA_TK))
    ya = _swa(qa, ka, va, lw["sink"])
    yb = _mla(qm, km, vt)
    x2, h2, afft = _out_proj(x.reshape(n, d), ya.reshape(n, SWA_WIDTH), yb.reshape(n, MLA_WIDTH), lw,
                             _pick(n, 512))
    cap = CAPACITY_FACTOR * n // N_EXPERTS
    g = n // LANES
    idx, gate, pos, rowpre = _select(afft, cap)
    ye = _ffn(idx.reshape(-1), h2, lw["w_gu"], lw["w_down"], gate.reshape(N_EXPERTS, cap, 1), cap)
    pos_t = pos.reshape(N_EXPERTS, n).T
    bounds = jnp.concatenate([rowpre.reshape(N_EXPERTS, g), jnp.full((N_EXPERTS, 1), cap, F32)], axis=1)
    x2 = _combine(bounds.astype(jnp.int32).reshape(-1), x2, pos_t, ye, g)
    return x2.reshape(b, s, d)


def _trunk(x, layers, g_final):
    b, s, d = x.shape
    cos128, sin128 = _rope_tables(s)
    for lw in layers:
        x = _layer(x, lw, cos128, sin128)
    return _final_norm(x.reshape(b * s, d), g_final[None], _pick(b * s, 512)).reshape(b, s, d)


def kernel(x_prompt, x_sample, g_attn, w_in, sink, g_cq, w_uq, g_ckv, w_uk, w_uv, g_ya, g_yb, w_out,
           g_ffn, w_router, w_gate, w_up, w_down, g_final):
    depth = w_in.shape[0]
    layers = [_prep_layer(l, g_attn, w_in, sink, g_cq, w_uq, g_ckv, w_uk, w_uv, g_ya, g_yb, w_out,
                          g_ffn, w_router, w_gate, w_up, w_down) for l in range(depth)]
    return _trunk(x_prompt, layers, g_final), _trunk(x_sample, layers, g_final)
```

```python
import functools

import numpy as np
import jax
import jax.numpy as jnp
from jax import lax
from jax.experimental import pallas as pl
from jax.experimental.pallas import tpu as pltpu

F32 = jnp.float32
BF16 = jnp.bfloat16

D_MODEL = 2048
HEAD_DIM = 128
SWA_HEADS = 8
SWA_KV_HEADS = 2
SWA_GROUP = SWA_HEADS // SWA_KV_HEADS
WINDOW = 128
BLOCK = 128
MLA_HEADS = 8
QK_NOPE = 128
QK_ROPE = 64
V_DIM = 128
Q_LORA = 512
KV_LORA = 512
ROPE_THETA = 10000.0
SWA_WIDTH = SWA_HEADS * HEAD_DIM
SWA_KV_WIDTH = SWA_KV_HEADS * HEAD_DIM
MLA_WIDTH = MLA_HEADS * V_DIM
N_EXPERTS = 16
D_EXPERT = 1408
CAPACITY_FACTOR = 2
EPS = 1e-6

OFF_QA = 0
OFF_KA = OFF_QA + SWA_WIDTH
OFF_VA = OFF_KA + SWA_KV_WIDTH
OFF_CQ = OFF_VA + SWA_KV_WIDTH
OFF_CKV = OFF_CQ + Q_LORA
OFF_KR = OFF_CKV + KV_LORA
OFF_KRR = OFF_KR + 128
IN_EXT = OFF_KRR + 128
MLA_QK = 256

SWA_SCALE = HEAD_DIM ** -0.5
MLA_SCALE = (QK_NOPE + QK_ROPE) ** -0.5 * float(np.log2(np.e))
NEG_BIG = -1e30
MLA_TQ = 512
MLA_TK = 512

V7X_VMEM_BYTES = 64 * 1024 * 1024
VMEM_LIMIT = V7X_VMEM_BYTES - 8 * 1024 * 1024


def _cparams(*sem):
    return pltpu.CompilerParams(dimension_semantics=sem, vmem_limit_bytes=VMEM_LIMIT)


def _const_spec(shape):
    nd = len(shape)
    return pl.BlockSpec(shape, lambda *_: (0,) * nd, pipeline_mode=pl.Buffered(1))


def _rms(x, g):
    return x * lax.rsqrt(jnp.mean(x * x, axis=-1, keepdims=True) + EPS) * g


def _dot(a, b):
    return jnp.dot(a, b, preferred_element_type=F32)


def _dot_nt(a, b):
    return lax.dot_general(a, b, (((1,), (1,)), ((), ())), preferred_element_type=F32)


def _in_proj_kernel(x_ref, g_ref, win_ref, gcq_ref, wqmt_ref, wqrt_ref, gckv_ref, wuk_ref, wuvt_ref,
                    cos_ref, sin_ref, cost_ref, sint_ref, qa_ref, ka_ref, va_ref, qmt_ref, km_ref, vt_ref):
    h = _rms(x_ref[...], g_ref[...]).astype(BF16)
    proj = _dot(h, win_ref[...])
    qa_ref[...] = (proj[:, OFF_QA:OFF_KA] * SWA_SCALE).astype(BF16)
    ka_ref[...] = proj[:, OFF_KA:OFF_VA].astype(BF16)
    va_ref[...] = proj[:, OFF_VA:OFF_CQ].astype(BF16)
    cos = cos_ref[...]
    sin = sin_ref[...]
    krope = (proj[:, OFF_KR:OFF_KRR] * cos + proj[:, OFF_KRR:IN_EXT] * sin).astype(BF16)
    cqn = _rms(proj[:, OFF_CQ:OFF_CKV], gcq_ref[...]).astype(BF16)
    qmain = _dot_nt(wqmt_ref[...], cqn)
    qrot = _dot_nt(wqrt_ref[...], cqn)
    cos_t = cost_ref[...]
    sin_t = sint_ref[...]
    for hh in range(MLA_HEADS):
        lo = hh * MLA_QK
        qmt_ref[lo:lo + QK_NOPE, :] = (qmain[lo:lo + QK_NOPE] * MLA_SCALE).astype(BF16)
        rp = qmain[lo + QK_NOPE:lo + MLA_QK] * cos_t + qrot[hh * 128:(hh + 1) * 128] * sin_t
        qmt_ref[lo + QK_NOPE:lo + MLA_QK, :] = (rp * MLA_SCALE).astype(BF16)
    ckvn = _rms(proj[:, OFF_CKV:OFF_KR], gckv_ref[...]).astype(BF16)
    kn = _dot(ckvn, wuk_ref[...])
    for hh in range(MLA_HEADS):
        lo = hh * MLA_QK
        km_ref[:, lo:lo + QK_NOPE] = kn[:, hh * QK_NOPE:(hh + 1) * QK_NOPE].astype(BF16)
        km_ref[:, lo + QK_NOPE:lo + MLA_QK] = krope
    vt_ref[...] = _dot_nt(wuvt_ref[...], ckvn).astype(BF16)


def _in_proj(x, lw, tables, tm):
    b, s, d = x.shape
    row = lambda w: pl.BlockSpec((None, tm, w), lambda i, j: (i, j, 0))
    tab = pl.BlockSpec((tm, 128), lambda i, j: (j, 0))
    tab_t = pl.BlockSpec((128, tm), lambda i, j: (0, j))
    out_shapes = (
        jax.ShapeDtypeStruct((b, s, SWA_WIDTH), BF16),
        jax.ShapeDtypeStruct((b, s, SWA_KV_WIDTH), BF16),
        jax.ShapeDtypeStruct((b, s, SWA_KV_WIDTH), BF16),
        jax.ShapeDtypeStruct((b, MLA_HEADS * MLA_QK, s), BF16),
        jax.ShapeDtypeStruct((b, s, MLA_HEADS * MLA_QK), BF16),
        jax.ShapeDtypeStruct((b, s // tm, MLA_WIDTH, tm), BF16),
    )
    return pl.pallas_call(
        _in_proj_kernel,
        grid=(b, s // tm),
        in_specs=[
            row(d),
            _const_spec((1, d)),
            _const_spec((d, IN_EXT)),
            _const_spec((1, Q_LORA)),
            _const_spec((MLA_HEADS * MLA_QK, Q_LORA)),
            _const_spec((MLA_HEADS * 128, Q_LORA)),
            _const_spec((1, KV_LORA)),
            _const_spec((KV_LORA, MLA_HEADS * QK_NOPE)),
            _const_spec((MLA_WIDTH, KV_LORA)),
            tab, tab, tab_t, tab_t,
        ],
        out_specs=(row(SWA_WIDTH), row(SWA_KV_WIDTH), row(SWA_KV_WIDTH),
                   pl.BlockSpec((None, MLA_HEADS * MLA_QK, tm), lambda i, j: (i, 0, j)),
                   row(MLA_HEADS * MLA_QK),
                   pl.BlockSpec((None, None, MLA_WIDTH, tm), lambda i, j: (i, j, 0, 0))),
        out_shape=out_shapes,
        compiler_params=_cparams("parallel", "parallel"),
    )(x, lw["g_attn"], lw["w_in"], lw["g_cq"], lw["w_uq_main_t"], lw["w_uq_rot_t"], lw["g_ckv"],
      lw["w_uk"], lw["w_uvt"], *tables)


def _alibi_slopes():
    return [float(np.float32(2.0 ** (-8.0 * (i + 1) / SWA_HEADS))) for i in range(SWA_HEADS)]


def _swa_kernel(sink_ref, q_ref, kp_ref, kc_ref, kn_ref, vp_ref, vc_ref, vn_ref, o_ref, *, seq):
    n = pl.program_id(1)
    qi = lax.broadcasted_iota(jnp.int32, (BLOCK, 3 * BLOCK), 0)
    kj = lax.broadcasted_iota(jnp.int32, (BLOCK, 3 * BLOCK), 1)
    rel = kj - BLOCK - qi
    spos = n * BLOCK + kj - BLOCK
    valid = (jnp.abs(rel) <= WINDOW) & (spos >= 0) & (spos < seq)
    dist = jnp.abs(rel).astype(F32)
    slopes = _alibi_slopes()
    kall = jnp.concatenate([kp_ref[...], kc_ref[...], kn_ref[...]], axis=0)
    vall = jnp.concatenate([vp_ref[...], vc_ref[...], vn_ref[...]], axis=0)
    for kvh in range(SWA_KV_HEADS):
        kh = kall[:, kvh * HEAD_DIM:(kvh + 1) * HEAD_DIM]
        vh = vall[:, kvh * HEAD_DIM:(kvh + 1) * HEAD_DIM]
        for g in range(SWA_GROUP):
            hd = kvh * SWA_GROUP + g
            q = q_ref[:, hd * HEAD_DIM:(hd + 1) * HEAD_DIM]
            logits = _dot_nt(q, kh)
            logits = jnp.where(valid, logits - slopes[hd] * dist, NEG_BIG)
            sk = sink_ref[hd]
            m = jnp.maximum(jnp.max(logits, axis=-1, keepdims=True), sk)
            p = jnp.exp(logits - m)
            denom = jnp.sum(p, axis=-1, keepdims=True) + jnp.exp(sk - m)
            o = _dot(p.astype(BF16), vh) / denom
            o_ref[:, hd * HEAD_DIM:(hd + 1) * HEAD_DIM] = o.astype(BF16)


def _swa(qa, ka, va, sink):
    b, s, _ = qa.shape
    nb = s // BLOCK
    kv = lambda f: pl.BlockSpec((None, BLOCK, SWA_KV_WIDTH), f)
    prev = lambda i, j: (i, jnp.maximum(j - 1, 0), 0)
    cur = lambda i, j: (i, j, 0)
    nxt = lambda i, j: (i, jnp.minimum(j + 1, nb - 1), 0)
    return pl.pallas_call(
        functools.partial(_swa_kernel, seq=s),
        grid=(b, nb),
        in_specs=[pl.BlockSpec(memory_space=pltpu.SMEM),
                  pl.BlockSpec((None, BLOCK, SWA_WIDTH), cur),
                  kv(prev), kv(cur), kv(nxt), kv(prev), kv(cur), kv(nxt)],
        out_specs=pl.BlockSpec((None, BLOCK, SWA_WIDTH), cur),
        out_shape=jax.ShapeDtypeStruct((b, s, SWA_WIDTH), BF16),
        compiler_params=_cparams("parallel", "parallel"),
    )(sink, qa, ka, ka, ka, va, va, va)


MLA_NSUB = 2
MLA_SUM_ROWS = 16


def _mla_kernel(q_ref, k_ref, vt_ref, o_ref, st_ref):
    nsub, _, tk, tq = st_ref.shape
    nchunk = vt_ref.shape[0]
    assert nchunk % 2 == 0
    qs = [q_ref[:, t * tq:(t + 1) * tq] for t in range(nsub)]

    def produce(t, slot, c):
        k0 = pl.multiple_of(c * tk, tk)
        st = _dot(k_ref[pl.ds(k0, tk), :], qs[t])
        st_ref[t, slot] = st
        return jnp.max(st, axis=0, keepdims=True)

    ones_rows = jnp.ones((MLA_SUM_ROWS, tk), BF16)

    def consume(t, slot, c, mx, state):
        m, acc = state
        m_new = jnp.maximum(m, mx)
        alpha = jnp.exp2(m - m_new)
        p = jnp.exp2(st_ref[t, slot] - m_new)
        vext = jnp.concatenate([vt_ref[c], ones_rows], axis=0)
        acc = alpha * acc + _dot(vext, p.astype(BF16))
        return m_new, acc

    def pair(c, mx0, states, produce_next):
        mx1 = [produce(t, 1, c + 1) for t in range(nsub)]
        states = [consume(t, 0, c, mx0[t], states[t]) for t in range(nsub)]
        if produce_next:
            mx0 = [produce(t, 0, c + 2) for t in range(nsub)]
        states = [consume(t, 1, c + 1, mx1[t], states[t]) for t in range(nsub)]
        return mx0, states

    def body(i, carry):
        return pair(2 * i, *carry, produce_next=True)

    state0 = (jnp.full((1, tq), NEG_BIG, F32), jnp.zeros((V_DIM + MLA_SUM_ROWS, tq), F32))
    init = ([produce(t, 0, 0) for t in range(nsub)], [state0] * nsub)
    mx0, states = lax.fori_loop(0, nchunk // 2 - 1, body, init, unroll=2)
    _, states = pair(nchunk - 2, mx0, states, produce_next=False)
    for t in range(nsub):
        acc = states[t][1]
        o_ref[t * tq:(t + 1) * tq, :] = (acc[:V_DIM] / acc[V_DIM:V_DIM + 1]).T.astype(BF16)


def _mla(qmt, km, vt):
    b, s, _ = km.shape
    nchunk, tk = vt.shape[1], vt.shape[3]
    tq = _pick(s, MLA_TQ)
    nsub = MLA_NSUB if s % (MLA_NSUB * tq) == 0 else 1
    return pl.pallas_call(
        _mla_kernel,
        grid=(b, MLA_HEADS, s // (nsub * tq)),
        in_specs=[pl.BlockSpec((None, MLA_QK, nsub * tq), lambda bb, h, i: (bb, h, i)),
                  pl.BlockSpec((None, s, MLA_QK), lambda bb, h, i: (bb, 0, h)),
                  pl.BlockSpec((None, nchunk, V_DIM, tk), lambda bb, h, i: (bb, 0, h, 0))],
        out_specs=pl.BlockSpec((None, nsub * tq, V_DIM), lambda bb, h, i: (bb, i, h)),
        out_shape=jax.ShapeDtypeStruct((b, s, MLA_WIDTH), BF16),
        scratch_shapes=[pltpu.VMEM((nsub, 2, tk, tq), F32)],
        compiler_params=_cparams("parallel", "parallel", "arbitrary"),
    )(qmt, km, vt)


def _out_proj_kernel(x_ref, ya_ref, yb_ref, gya_ref, gyb_ref, woa_ref, wob_ref, gffn_ref, wrt_ref,
                     xo_ref, h2_ref, afft_ref):
    yan = _rms(ya_ref[...].astype(F32), gya_ref[...]).astype(BF16)
    ybn = _rms(yb_ref[...].astype(F32), gyb_ref[...]).astype(BF16)
    xn = x_ref[...] + _dot(yan, woa_ref[...]) + _dot(ybn, wob_ref[...])
    xo_ref[...] = xn
    h2 = _rms(xn, gffn_ref[...]).astype(BF16)
    half = h2.shape[1] // 2
    hi = pltpu.bitcast(h2[:, :half].astype(F32), jnp.uint32)
    lo = pltpu.bitcast(h2[:, half:].astype(F32), jnp.uint32)
    h2_ref[...] = hi | (lo >> 16)
    logits = _dot_nt(wrt_ref[...], h2)
    m = jnp.max(logits, axis=0, keepdims=True)
    p = jnp.exp(logits - m)
    afft_ref[...] = p / jnp.sum(p, axis=0, keepdims=True)


def _out_proj(x2, ya2, yb2, lw, tm):
    n, d = x2.shape
    row = lambda w: pl.BlockSpec((tm, w), lambda i: (i, 0))
    return pl.pallas_call(
        _out_proj_kernel,
        grid=(n // tm,),
        in_specs=[row(d), row(SWA_WIDTH), row(MLA_WIDTH),
                  _const_spec((1, SWA_WIDTH)), _const_spec((1, MLA_WIDTH)),
                  _const_spec((SWA_WIDTH, d)), _const_spec((MLA_WIDTH, d)),
                  _const_spec((1, d)), _const_spec((N_EXPERTS, d))],
        out_specs=(row(d), row(d // 2), pl.BlockSpec((N_EXPERTS, tm), lambda i: (0, i))),
        out_shape=(jax.ShapeDtypeStruct((n, d), F32), jax.ShapeDtypeStruct((n, d // 2), jnp.uint32),
                   jax.ShapeDtypeStruct((N_EXPERTS, n), F32)),
        compiler_params=_cparams("parallel"),
    )(x2, ya2, yb2, lw["g_ya"], lw["g_yb"], lw["w_out_a"], lw["w_out_b"], lw["g_ffn"], lw["w_router_t"])


LANES = 128
SEL_CJ = 1024


def _token_cumsum(mask, upper, ones, lstrict):
    mb = mask.astype(BF16)
    within = _dot(mb, upper)
    tot = _dot(mb, ones)
    rowpre = _dot(lstrict, tot.astype(BF16))
    return within, tot, rowpre


def _select_kernel(a_ref, idx_ref, gate_ref, pos_ref, rowpre_ref, *, cap):
    a = a_ref[...]
    g = a.shape[0]
    bits = pltpu.bitcast(a, jnp.int32)

    def radix_step(i, t):
        cand = t | jnp.left_shift(jnp.int32(1), 30 - i)
        cnt = jnp.sum(jnp.where(bits >= cand, 1.0, 0.0), keepdims=True)
        return jnp.where(cnt >= cap, cand, t)

    thr = lax.fori_loop(0, 31, radix_step, jnp.zeros((1, 1), jnp.int32))
    ii = lax.broadcasted_iota(jnp.int32, (LANES, LANES), 0)
    jj = lax.broadcasted_iota(jnp.int32, (LANES, LANES), 1)
    upper = jnp.where(ii <= jj, 1.0, 0.0).astype(BF16)
    ones = jnp.ones((LANES, LANES), BF16)
    gi = lax.broadcasted_iota(jnp.int32, (g, g), 0)
    gj = lax.broadcasted_iota(jnp.int32, (g, g), 1)
    lstrict = jnp.where(gj < gi, 1.0, 0.0).astype(BF16)

    gt = bits > thr
    eq = bits == thr
    need = cap - jnp.sum(jnp.where(gt, 1.0, 0.0), keepdims=True)
    eqf = jnp.where(eq, 1.0, 0.0)
    w_eq, _, rp_eq = _token_cumsum(eqf, upper, ones, lstrict)
    tie_rank = w_eq - eqf + rp_eq
    self_ = jnp.where(gt, 1.0, jnp.where(tie_rank < need, eqf, 0.0))
    w, tot, rp = _token_cumsum(self_, upper, ones, lstrict)
    pos_ref[...] = jnp.where(self_ > 0.0, w - 1.0 + rp, -1.0)
    rpt = rp.T
    rowpre_ref[...] = rpt[0:1, :]

    rc = rp + tot
    wt = w.T.astype(BF16)
    pre_hi = jnp.floor(rpt[0:16] * (1.0 / 64.0))
    pre_lo = rpt[0:16] - 64.0 * pre_hi
    pre_hi = pre_hi.astype(BF16)
    pre_lo = pre_lo.astype(BF16)
    at = a.T
    a_h = at.astype(BF16)
    r1 = at - a_h.astype(F32)
    a_m = r1.astype(BF16)
    a_l = (r1 - a_m.astype(F32)).astype(BF16)
    cj_n = min(SEL_CJ, cap)
    for jc in range(cap // cj_n):
        j = (lax.broadcasted_iota(jnp.int32, (1, cj_n), 1) + jc * cj_n).astype(F32)
        rct = jnp.tile(rc, (1, cj_n // LANES))
        row = jnp.sum(jnp.where(rct <= j, 1.0, 0.0), axis=0, keepdims=True)
        grow = lax.broadcasted_iota(jnp.int32, (g, cj_n), 0).astype(F32)
        onehot = jnp.where(grow == row, 1.0, 0.0).astype(BF16)
        wg = _dot(wt, onehot)
        pre = 64.0 * _dot(pre_hi, onehot)[0:1] + _dot(pre_lo, onehot)[0:1]
        jloc = j - pre
        lane = jnp.sum(jnp.where(wg <= jloc, 1.0, 0.0), axis=0, keepdims=True)
        idx_ref[:, jc * cj_n:(jc + 1) * cj_n] = (row * LANES + lane).astype(jnp.int32)
        ag = _dot(a_h, onehot) + _dot(a_m, onehot) + _dot(a_l, onehot)
        lrow = lax.broadcasted_iota(jnp.int32, (LANES, cj_n), 0).astype(F32)
        gate_ref[:, jc * cj_n:(jc + 1) * cj_n] = jnp.sum(jnp.where(lrow == lane, ag, 0.0), axis=0,
                                                          keepdims=True)


def _select(afft, cap):
    e, n = afft.shape
    g = n // LANES
    blk = lambda w: pl.BlockSpec((None, 1, w), lambda i: (i, 0, 0))
    tok = pl.BlockSpec((None, g, LANES), lambda i: (i, 0, 0))
    return pl.pallas_call(
        functools.partial(_select_kernel, cap=cap),
        grid=(e,),
        in_specs=[tok],
        out_specs=(blk(cap), blk(cap), tok, blk(g)),
        out_shape=(jax.ShapeDtypeStruct((e, 1, cap), jnp.int32), jax.ShapeDtypeStruct((e, 1, cap), F32),
                   jax.ShapeDtypeStruct((e, g, LANES), F32), jax.ShapeDtypeStruct((e, 1, g), F32)),
        compiler_params=_cparams("parallel"),
    )(afft.reshape(e, g, LANES))


FFN_TT = 256


def _ffn_kernel(idx_ref, h2_hbm, wgu_ref, wd_ref, gate_ref, ye_ref, xbuf, sem):
    tt = xbuf.shape[1]
    nj = pl.num_programs(1)
    step = pl.program_id(0) * nj + pl.program_id(1)
    last = pl.num_programs(0) * nj - 1
    slot = step % 2

    def row_copy(tok, sl, r):
        return pltpu.make_async_copy(h2_hbm.at[pl.ds(tok, 1), :], xbuf.at[sl, pl.ds(r, 1), :], sem.at[sl])

    def issue(s, sl):
        for r in range(tt):
            row_copy(idx_ref[s * tt + r], sl, r).start()

    def wait_slot(sl):
        pltpu.make_async_copy(h2_hbm.at[pl.ds(0, tt), :], xbuf.at[sl], sem.at[sl]).wait()

    @pl.when(step == 0)
    def _():
        issue(0, 0)

    wait_slot(slot)
    xp = xbuf[slot]
    x = jnp.concatenate([pltpu.bitcast(xp & jnp.uint32(0xFFFF0000), F32).astype(BF16),
                         pltpu.bitcast(xp << 16, F32).astype(BF16)], axis=1)
    issue(jnp.minimum(step + 1, last), 1 - slot)
    au = _dot(x, wgu_ref[...])
    a = au[:, :D_EXPERT]
    u = au[:, D_EXPERT:]
    act = (a / (1.0 + jnp.exp(-a)) * u).astype(BF16)
    ye_ref[...] = (_dot(act, wd_ref[...]) * gate_ref[...]).astype(BF16)

    @pl.when(step == last)
    def _():
        wait_slot(1 - slot)


def _ffn(idx_flat, h2p, wgu, wd, gate, cap):
    e, d = wgu.shape[0], wgu.shape[1]
    tt = _pick(cap, FFN_TT)
    return pl.pallas_call(
        _ffn_kernel,
        grid_spec=pltpu.PrefetchScalarGridSpec(
            num_scalar_prefetch=1,
            grid=(e, cap // tt),
            in_specs=[pl.BlockSpec(memory_space=pl.ANY),
                      pl.BlockSpec((None, d, 2 * D_EXPERT), lambda i, j, idx: (i, 0, 0)),
                      pl.BlockSpec((None, D_EXPERT, d), lambda i, j, idx: (i, 0, 0)),
                      pl.BlockSpec((None, tt, 1), lambda i, j, idx: (i, j, 0))],
            out_specs=pl.BlockSpec((None, tt, d), lambda i, j, idx: (i, j, 0)),
            scratch_shapes=[pltpu.VMEM((2, tt, d // 2), jnp.uint32), pltpu.SemaphoreType.DMA((2,))]),
        out_shape=jax.ShapeDtypeStruct((e, cap, d), BF16),
        compiler_params=_cparams("arbitrary", "arbitrary"),
    )(idx_flat, h2p, wgu, wd, gate)


CMB_TT = 128
CMB_BLK = 16
CMB_KC = 256
CMB_ROWS = N_EXPERTS * CMB_TT + N_EXPERTS * 2 * CMB_BLK + CMB_KC


def _combine_kernel(rowpre_ref, x_ref, pos_ref, ye_hbm, o_ref, buf, sem, acc_ref, *, g):
    i = pl.program_id(0)
    last = pl.num_programs(0) - 1
    slot = i % 2

    def plan(tile):
        starts, nblks, offs = [], [], []
        off = jnp.int32(0)
        for e in range(N_EXPERTS):
            lo = rowpre_ref[e * (g + 1) + tile]
            hi = rowpre_ref[e * (g + 1) + tile + 1]
            start = (lo >> 4) << 4
            nblk = jnp.where(hi > lo, (hi - start + (CMB_BLK - 1)) >> 4, 0)
            starts.append(start)
            nblks.append(nblk)
            offs.append(off)
            off = off + nblk
        return starts, nblks, offs, off

    def block_copy(e, src_row, dst_row, sl):
        return pltpu.make_async_copy(ye_hbm.at[e, pl.ds(src_row, CMB_BLK), :],
                                     buf.at[sl, pl.ds(dst_row, CMB_BLK), :], sem.at[sl])

    def issue(tile, sl):
        starts, nblks, offs, _ = plan(tile)
        for e in range(N_EXPERTS):
            def body(b, c, e=e):
                block_copy(e, pl.multiple_of(starts[e] + b * CMB_BLK, CMB_BLK),
                           pl.multiple_of((offs[e] + b) * CMB_BLK, CMB_BLK), sl).start()
                return c
            lax.fori_loop(0, nblks[e], body, 0)

    def wait(tile, sl):
        total = plan(tile)[3]

        def body(b, c):
            block_copy(0, 0, 0, sl).wait()
            return c
        lax.fori_loop(0, total, body, 0)

    @pl.when(i == 0)
    def _():
        buf[...] = jnp.zeros_like(buf)
        issue(0, 0)

    @pl.when(i < last)
    def _():
        issue(i + 1, 1 - slot)

    wait(i, slot)
    starts, nblks, offs, total = plan(i)
    pos = jnp.concatenate([pos_ref[...], jnp.zeros((LANES - N_EXPERTS, CMB_TT), F32)], axis=0).T
    cols = []
    for e in range(N_EXPERTS):
        pe = pos[:, e:e + 1]
        shift = (offs[e] * CMB_BLK - starts[e]).astype(F32)
        col = jnp.where(pe >= 0.0, pe + shift, -1.0)
        cols.append(jnp.broadcast_to(col, (CMB_TT, CMB_KC)))
    acc_ref[...] = x_ref[...]

    def chunk(k, c):
        k0 = pl.multiple_of(k * CMB_KC, CMB_KC)
        r = (lax.broadcasted_iota(jnp.int32, (CMB_TT, CMB_KC), 1) + k0).astype(F32)
        hit = jnp.where(cols[0] == r, 1.0, 0.0)
        for e in range(1, N_EXPERTS):
            hit = hit + jnp.where(cols[e] == r, 1.0, 0.0)
        acc_ref[...] += _dot(hit.astype(BF16), buf[slot, pl.ds(k0, CMB_KC), :])
        return c

    lax.fori_loop(0, (total * CMB_BLK + (CMB_KC - 1)) // CMB_KC, chunk, 0)
    o_ref[...] = acc_ref[...]


def _combine(rowpre_flat, x2, pos, ye, g):
    n, d = x2.shape
    assert n == g * CMB_TT
    return pl.pallas_call(
        functools.partial(_combine_kernel, g=g),
        grid_spec=pltpu.PrefetchScalarGridSpec(
            num_scalar_prefetch=1,
            grid=(g,),
            in_specs=[pl.BlockSpec((CMB_TT, d), lambda i, rp: (i, 0)),
                      pl.BlockSpec((N_EXPERTS, CMB_TT), lambda i, rp: (0, i)),
                      pl.BlockSpec(memory_space=pl.ANY)],
            out_specs=pl.BlockSpec((CMB_TT, d), lambda i, rp: (i, 0)),
            scratch_shapes=[pltpu.VMEM((2, CMB_ROWS, d), BF16), pltpu.SemaphoreType.DMA((2,)),
                            pltpu.VMEM((CMB_TT, d), F32)]),
        out_shape=jax.ShapeDtypeStruct((n, d), F32),
        compiler_params=_cparams("arbitrary"),
    )(rowpre_flat, x2, pos, ye)


def _final_norm_kernel(x_ref, g_ref, o_ref):
    o_ref[...] = _rms(x_ref[...], g_ref[...])


def _final_norm(x2, g, tm):
    n, d = x2.shape
    return pl.pallas_call(
        _final_norm_kernel,
        grid=(n // tm,),
        in_specs=[pl.BlockSpec((tm, d), lambda i: (i, 0)), _const_spec((1, d))],
        out_specs=pl.BlockSpec((tm, d), lambda i: (i, 0)),
        out_shape=jax.ShapeDtypeStruct((n, d), F32),
        compiler_params=_cparams("parallel"),
    )(x2, g)


def _rot_half_cols(w):
    half = w.shape[-1] // 2
    return jnp.concatenate([-w[..., half:], w[..., :half]], axis=-1)


def _prep_layer(l, g_attn, w_in, sink, g_cq, w_uq, g_ckv, w_uk, w_uv, g_ya, g_yb, w_out, g_ffn,
                w_router, w_gate, w_up, w_down):
    wi = w_in[l]
    w_kr = wi[:, OFF_KR:OFF_KR + QK_ROPE]
    zpad = jnp.zeros((D_MODEL, 128 - QK_ROPE), F32)
    w_in_ext = jnp.concatenate([wi[:, :OFF_KR], w_kr, zpad, _rot_half_cols(w_kr), zpad], axis=1)
    wq = w_uq[l].reshape(Q_LORA, MLA_HEADS, QK_NOPE + QK_ROPE)
    wq_rope = wq[:, :, QK_NOPE:]
    zq = jnp.zeros((Q_LORA, MLA_HEADS, 128 - QK_ROPE), F32)
    w_uq_main = jnp.concatenate([wq, zq], axis=-1).reshape(Q_LORA, MLA_HEADS * MLA_QK)
    w_uq_rot = jnp.concatenate([_rot_half_cols(wq_rope), zq], axis=-1).reshape(Q_LORA, MLA_HEADS * 128)
    return dict(
        g_attn=g_attn[l][None], w_in=w_in_ext.astype(BF16), sink=sink[l],
        g_cq=g_cq[l][None], w_uq_main_t=w_uq_main.T.astype(BF16), w_uq_rot_t=w_uq_rot.T.astype(BF16),
        g_ckv=g_ckv[l][None], w_uk=w_uk[l].astype(BF16), w_uvt=w_uv[l].T.astype(BF16),
        g_ya=g_ya[l][None], g_yb=g_yb[l][None],
        w_out_a=w_out[l][:SWA_WIDTH].astype(BF16), w_out_b=w_out[l][SWA_WIDTH:].astype(BF16),
        g_ffn=g_ffn[l][None], w_router_t=w_router[l].T.astype(BF16),
        w_gu=jnp.concatenate([w_gate[l], w_up[l]], axis=-1).astype(BF16),
        w_down=w_down[l].astype(BF16),
    )


def _rope_tables(s):
    inv = 1.0 / (ROPE_THETA ** (jnp.arange(0, QK_ROPE, 2, dtype=F32) / QK_ROPE))
    ang = jnp.arange(s, dtype=F32)[:, None] * inv[None, :]
    z = jnp.zeros((s, 128 - QK_ROPE), F32)
    cos, sin = jnp.cos(ang), jnp.sin(ang)
    cos128 = jnp.concatenate([cos, cos, z], axis=1)
    sin128 = jnp.concatenate([sin, sin, z], axis=1)
    return cos128, sin128, cos128.T, sin128.T


def _pick(n, pref):
    t = pref
    while n % t:
        t //= 2
    return t


def _layer(x, lw, tables):
    b, s, d = x.shape
    n = b * s
    qa, ka, va, qmt, km, vt = _in_proj(x, lw, tables, _pick(s, MLA_TK))
    ya = _swa(qa, ka, va, lw["sink"])
    yb = _mla(qmt, km, vt)
    x2, h2, afft = _out_proj(x.reshape(n, d), ya.reshape(n, SWA_WIDTH), yb.reshape(n, MLA_WIDTH), lw,
                             _pick(n, 512))
    cap = CAPACITY_FACTOR * n // N_EXPERTS
    g = n // LANES
    idx, gate, pos, rowpre = _select(afft, cap)
    ye = _ffn(idx.reshape(-1), h2, lw["w_gu"], lw["w_down"], gate.reshape(N_EXPERTS, cap, 1), cap)
    bounds = jnp.concatenate([rowpre.reshape(N_EXPERTS, g), jnp.full((N_EXPERTS, 1), cap, F32)], axis=1)
    x2 = _combine(bounds.astype(jnp.int32).reshape(-1), x2, pos.reshape(N_EXPERTS, n), ye, g)
    return x2.reshape(b, s, d)


def _trunk(x, layers, g_final):
    b, s, d = x.shape
    tables = _rope_tables(s)
    for lw in layers:
        x = _layer(x, lw, tables)
    return _final_norm(x.reshape(b * s, d), g_final[None], _pick(b * s, 512)).reshape(b, s, d)


def kernel(x_prompt, x_sample, g_attn, w_in, sink, g_cq, w_uq, g_ckv, w_uk, w_uv, g_ya, g_yb, w_out,
           g_ffn, w_router, w_gate, w_up, w_down, g_final):
    depth = w_in.shape[0]
    layers = [_prep_layer(l, g_attn, w_in, sink, g_cq, w_uq, g_ckv, w_uk, w_uv, g_ya, g_yb, w_out,
                          g_ffn, w_router, w_gate, w_up, w_down) for l in range(depth)]
    return _trunk(x_prompt, layers, g_final), _trunk(x_sample, layers, g_final)
```

```python
import functools

import numpy as np
import jax
import jax.numpy as jnp
from jax import lax
from jax.experimental import pallas as pl
from jax.experimental.pallas import tpu as pltpu

F32 = jnp.float32
BF16 = jnp.bfloat16

D_MODEL = 2048
HEAD_DIM = 128
SWA_HEADS = 8
SWA_KV_HEADS = 2
SWA_GROUP = SWA_HEADS // SWA_KV_HEADS
WINDOW = 128
BLOCK = 128
MLA_HEADS = 8
QK_NOPE = 128
QK_ROPE = 64
V_DIM = 128
Q_LORA = 512
KV_LORA = 512
ROPE_THETA = 10000.0
SWA_WIDTH = SWA_HEADS * HEAD_DIM
SWA_KV_WIDTH = SWA_KV_HEADS * HEAD_DIM
MLA_WIDTH = MLA_HEADS * V_DIM
N_EXPERTS = 16
D_EXPERT = 1408
CAPACITY_FACTOR = 2
EPS = 1e-6
LANES = 128

SRC_QA = 0
SRC_KA = SRC_QA + SWA_WIDTH
SRC_VA = SRC_KA + SWA_KV_WIDTH
SRC_CQ = SRC_VA + SWA_KV_WIDTH
SRC_KR = SRC_CQ + Q_LORA + KV_LORA
OFF_KA = 0
OFF_CQ = OFF_KA + SWA_KV_WIDTH
OFF_CKV = OFF_CQ + Q_LORA
OFF_KR = OFF_CKV + KV_LORA
OFF_KRR = OFF_KR + 128
IN_EXT = OFF_KRR + 128
TOFF_QA = 0
TOFF_VA = TOFF_QA + SWA_WIDTH
IN_T = TOFF_VA + SWA_KV_WIDTH
MLA_QK = 256

SWA_SCALE = HEAD_DIM ** -0.5
MLA_SCALE = (QK_NOPE + QK_ROPE) ** -0.5 * float(np.log2(np.e))
NEG_BIG = -1e30
MLA_TQ = 512
MLA_TK = 512

V7X_VMEM_BYTES = 64 * 1024 * 1024
VMEM_LIMIT = V7X_VMEM_BYTES - 8 * 1024 * 1024


def _cparams(*sem):
    return pltpu.CompilerParams(dimension_semantics=sem, vmem_limit_bytes=VMEM_LIMIT)


def _const_spec(shape):
    nd = len(shape)
    return pl.BlockSpec(shape, lambda *_: (0,) * nd, pipeline_mode=pl.Buffered(1))


def _rms(x, g):
    return x * lax.rsqrt(jnp.mean(x * x, axis=-1, keepdims=True) + EPS) * g


def _dot(a, b):
    return jnp.dot(a, b, preferred_element_type=F32)


def _dot_nt(a, b):
    return lax.dot_general(a, b, (((1,), (1,)), ((), ())), preferred_element_type=F32)


def _in_proj_kernel(x_ref, g_ref, win_ref, wint_ref, gcq_ref, wqmt_ref, wqrt_ref, gckv_ref, wuk_ref,
                    wuvt_ref, cos_ref, sin_ref, cost_ref, sint_ref,
                    qat_ref, ka_ref, vat_ref, qmt_ref, km_ref, vt_ref):
    h = _rms(x_ref[...], g_ref[...]).astype(BF16)
    proj = _dot(h, win_ref[...])
    proj_t = _dot_nt(wint_ref[...], h)
    qat_ref[...] = (proj_t[TOFF_QA:TOFF_VA] * SWA_SCALE).astype(BF16)
    vat_ref[...] = proj_t[TOFF_VA:IN_T].astype(BF16)
    ka_ref[...] = proj[:, OFF_KA:OFF_CQ].astype(BF16)
    cos = cos_ref[...]
    sin = sin_ref[...]
    krope = (proj[:, OFF_KR:OFF_KRR] * cos + proj[:, OFF_KRR:IN_EXT] * sin).astype(BF16)
    cqn = _rms(proj[:, OFF_CQ:OFF_CKV], gcq_ref[...]).astype(BF16)
    qmain = _dot_nt(wqmt_ref[...], cqn)
    qrot = _dot_nt(wqrt_ref[...], cqn)
    cos_t = cost_ref[...]
    sin_t = sint_ref[...]
    for hh in range(MLA_HEADS):
        lo = hh * MLA_QK
        qmt_ref[lo:lo + QK_NOPE, :] = (qmain[lo:lo + QK_NOPE] * MLA_SCALE).astype(BF16)
        rp = qmain[lo + QK_NOPE:lo + MLA_QK] * cos_t + qrot[hh * 128:(hh + 1) * 128] * sin_t
        qmt_ref[lo + QK_NOPE:lo + MLA_QK, :] = (rp * MLA_SCALE).astype(BF16)
    ckvn = _rms(proj[:, OFF_CKV:OFF_KR], gckv_ref[...]).astype(BF16)
    kn = _dot(ckvn, wuk_ref[...])
    for hh in range(MLA_HEADS):
        lo = hh * MLA_QK
        km_ref[:, lo:lo + QK_NOPE] = kn[:, hh * QK_NOPE:(hh + 1) * QK_NOPE].astype(BF16)
        km_ref[:, lo + QK_NOPE:lo + MLA_QK] = krope
    vt_ref[...] = _dot_nt(wuvt_ref[...], ckvn).astype(BF16)


def _in_proj(x, lw, tables, tm):
    b, s, d = x.shape
    row = lambda w: pl.BlockSpec((None, tm, w), lambda i, j: (i, j, 0))
    tab = pl.BlockSpec((tm, 128), lambda i, j: (j, 0))
    tab_t = pl.BlockSpec((128, tm), lambda i, j: (0, j))
    col = lambda w: pl.BlockSpec((None, w, tm), lambda i, j: (i, 0, j))
    out_shapes = (
        jax.ShapeDtypeStruct((b, SWA_WIDTH, s), BF16),
        jax.ShapeDtypeStruct((b, s, SWA_KV_WIDTH), BF16),
        jax.ShapeDtypeStruct((b, SWA_KV_WIDTH, s), BF16),
        jax.ShapeDtypeStruct((b, MLA_HEADS * MLA_QK, s), BF16),
        jax.ShapeDtypeStruct((b, s, MLA_HEADS * MLA_QK), BF16),
        jax.ShapeDtypeStruct((b, s // tm, MLA_WIDTH, tm), BF16),
    )
    return pl.pallas_call(
        _in_proj_kernel,
        grid=(b, s // tm),
        in_specs=[
            row(d),
            _const_spec((1, d)),
            _const_spec((d, IN_EXT)),
            _const_spec((IN_T, d)),
            _const_spec((1, Q_LORA)),
            _const_spec((MLA_HEADS * MLA_QK, Q_LORA)),
            _const_spec((MLA_HEADS * 128, Q_LORA)),
            _const_spec((1, KV_LORA)),
            _const_spec((KV_LORA, MLA_HEADS * QK_NOPE)),
            _const_spec((MLA_WIDTH, KV_LORA)),
            tab, tab, tab_t, tab_t,
        ],
        out_specs=(col(SWA_WIDTH), row(SWA_KV_WIDTH), col(SWA_KV_WIDTH), col(MLA_HEADS * MLA_QK),
                   row(MLA_HEADS * MLA_QK),
                   pl.BlockSpec((None, None, MLA_WIDTH, tm), lambda i, j: (i, j, 0, 0))),
        out_shape=out_shapes,
        compiler_params=_cparams("parallel", "parallel"),
    )(x, lw["g_attn"], lw["w_in"], lw["w_in_t"], lw["g_cq"], lw["w_uq_main_t"], lw["w_uq_rot_t"], lw["g_ckv"],
      lw["w_uk"], lw["w_uvt"], *tables)


def _alibi_slopes():
    return [float(np.float32(2.0 ** (-8.0 * (i + 1) / SWA_HEADS))) for i in range(SWA_HEADS)]


def _swa_kernel(sink_ref, qt_ref, kp_ref, kc_ref, kn_ref, vtp_ref, vtc_ref, vtn_ref, o_ref, *, seq):
    n = pl.program_id(1)
    ki = lax.broadcasted_iota(jnp.int32, (3 * BLOCK, BLOCK), 0)
    qj = lax.broadcasted_iota(jnp.int32, (3 * BLOCK, BLOCK), 1)
    rel = ki - BLOCK - qj
    spos = n * BLOCK + ki - BLOCK
    valid = (jnp.abs(rel) <= WINDOW) & (spos >= 0) & (spos < seq)
    dist = jnp.abs(rel).astype(F32)
    slopes = _alibi_slopes()
    kall = jnp.concatenate([kp_ref[...], kc_ref[...], kn_ref[...]], axis=0)
    vtall = jnp.concatenate([vtp_ref[...], vtc_ref[...], vtn_ref[...]], axis=1)
    for kvh in range(SWA_KV_HEADS):
        kh = kall[:, kvh * HEAD_DIM:(kvh + 1) * HEAD_DIM]
        vth = vtall[kvh * HEAD_DIM:(kvh + 1) * HEAD_DIM, :]
        heads = [kvh * SWA_GROUP + g for g in range(SWA_GROUP)]
        qt = jnp.concatenate([qt_ref[hd * HEAD_DIM:(hd + 1) * HEAD_DIM, :] for hd in heads], axis=1)
        raw = _dot(kh, qt)
        logits = jnp.concatenate(
            [jnp.where(valid, raw[:, g * BLOCK:(g + 1) * BLOCK] - slopes[hd] * dist, NEG_BIG)
             for g, hd in enumerate(heads)], axis=1)
        sk = jnp.concatenate([jnp.full((1, BLOCK), sink_ref[hd], F32) for hd in heads], axis=1)
        m = jnp.maximum(jnp.max(logits, axis=0, keepdims=True), sk)
        p = jnp.exp(logits - m)
        denom = jnp.sum(p, axis=0, keepdims=True) + jnp.exp(sk - m)
        ot = _dot(vth, p.astype(BF16)) / denom
        for g, hd in enumerate(heads):
            o_ref[:, hd * HEAD_DIM:(hd + 1) * HEAD_DIM] = ot[:, g * BLOCK:(g + 1) * BLOCK].T.astype(BF16)


def _swa(qat, ka, vat, sink):
    b, s, _ = ka.shape
    nb = s // BLOCK
    prev = lambda j: jnp.maximum(j - 1, 0)
    nxt = lambda j: jnp.minimum(j + 1, nb - 1)
    kspec = lambda f: pl.BlockSpec((None, BLOCK, SWA_KV_WIDTH), lambda i, j: (i, f(j), 0))
    vspec = lambda f: pl.BlockSpec((None, SWA_KV_WIDTH, BLOCK), lambda i, j: (i, 0, f(j)))
    cur = lambda j: j
    return pl.pallas_call(
        functools.partial(_swa_kernel, seq=s),
        grid=(b, nb),
        in_specs=[pl.BlockSpec(memory_space=pltpu.SMEM),
                  pl.BlockSpec((None, SWA_WIDTH, BLOCK), lambda i, j: (i, 0, j)),
                  kspec(prev), kspec(cur), kspec(nxt), vspec(prev), vspec(cur), vspec(nxt)],
        out_specs=pl.BlockSpec((None, BLOCK, SWA_WIDTH), lambda i, j: (i, j, 0)),
        out_shape=jax.ShapeDtypeStruct((b, s, SWA_WIDTH), BF16),
        compiler_params=_cparams("parallel", "parallel"),
    )(sink, qat, ka, ka, ka, vat, vat, vat)


MLA_NSUB = 2
MLA_SUM_ROWS = 16


def _mla_kernel(q_ref, k_ref, vt_ref, o_ref, st_ref):
    nsub, _, tk, tq = st_ref.shape
    nchunk = vt_ref.shape[0]
    assert nchunk % 2 == 0
    qs = [q_ref[:, t * tq:(t + 1) * tq] for t in range(nsub)]

    def produce(t, slot, c):
        k0 = pl.multiple_of(c * tk, tk)
        st = _dot(k_ref[pl.ds(k0, tk), :], qs[t])
        st_ref[t, slot] = st
        return jnp.max(st, axis=0, keepdims=True)

    ones_rows = jnp.ones((MLA_SUM_ROWS, tk), BF16)

    def consume(t, slot, c, mx, state):
        m, acc = state
        m_new = jnp.maximum(m, mx)
        alpha = jnp.exp2(m - m_new)
        p = jnp.exp2(st_ref[t, slot] - m_new)
        vext = jnp.concatenate([vt_ref[c], ones_rows], axis=0)
        acc = alpha * acc + _dot(vext, p.astype(BF16))
        return m_new, acc

    def pair(c, mx0, states, produce_next):
        mx1 = [produce(t, 1, c + 1) for t in range(nsub)]
        states = [consume(t, 0, c, mx0[t], states[t]) for t in range(nsub)]
        if produce_next:
            mx0 = [produce(t, 0, c + 2) for t in range(nsub)]
        states = [consume(t, 1, c + 1, mx1[t], states[t]) for t in range(nsub)]
        return mx0, states

    def body(i, carry):
        return pair(2 * i, *carry, produce_next=True)

    state0 = (jnp.full((1, tq), NEG_BIG, F32), jnp.zeros((V_DIM + MLA_SUM_ROWS, tq), F32))
    init = ([produce(t, 0, 0) for t in range(nsub)], [state0] * nsub)
    mx0, states = lax.fori_loop(0, nchunk // 2 - 1, body, init, unroll=2)
    _, states = pair(nchunk - 2, mx0, states, produce_next=False)
    for t in range(nsub):
        acc = states[t][1]
        o_ref[t * tq:(t + 1) * tq, :] = (acc[:V_DIM] / acc[V_DIM:V_DIM + 1]).T.astype(BF16)


def _mla(qmt, km, vt):
    b, s, _ = km.shape
    nchunk, tk = vt.shape[1], vt.shape[3]
    tq = _pick(s, MLA_TQ)
    nsub = MLA_NSUB if s % (MLA_NSUB * tq) == 0 else 1
    return pl.pallas_call(
        _mla_kernel,
        grid=(b, MLA_HEADS, s // (nsub * tq)),
        in_specs=[pl.BlockSpec((None, MLA_QK, nsub * tq), lambda bb, h, i: (bb, h, i)),
                  pl.BlockSpec((None, s, MLA_QK), lambda bb, h, i: (bb, 0, h)),
                  pl.BlockSpec((None, nchunk, V_DIM, tk), lambda bb, h, i: (bb, 0, h, 0))],
        out_specs=pl.BlockSpec((None, nsub * tq, V_DIM), lambda bb, h, i: (bb, i, h)),
        out_shape=jax.ShapeDtypeStruct((b, s, MLA_WIDTH), BF16),
        scratch_shapes=[pltpu.VMEM((nsub, 2, tk, tq), F32)],
        compiler_params=_cparams("parallel", "parallel", "arbitrary"),
    )(qmt, km, vt)


def _out_proj_kernel(x_ref, ya_ref, yb_ref, gya_ref, gyb_ref, woa_ref, wob_ref, gffn_ref, wrt_ref,
                     xo_ref, h2_ref, afft_ref):
    yan = _rms(ya_ref[...].astype(F32), gya_ref[...]).astype(BF16)
    ybn = _rms(yb_ref[...].astype(F32), gyb_ref[...]).astype(BF16)
    xn = x_ref[...] + _dot(yan, woa_ref[...]) + _dot(ybn, wob_ref[...])
    xo_ref[...] = xn
    h2 = _rms(xn, gffn_ref[...]).astype(BF16)
    half = h2.shape[1] // 2
    hi = pltpu.bitcast(h2[:, :half].astype(F32), jnp.uint32)
    lo = pltpu.bitcast(h2[:, half:].astype(F32), jnp.uint32)
    h2_ref[...] = hi | (lo >> 16)
    logits = _dot_nt(wrt_ref[...], h2)
    m = jnp.max(logits, axis=0, keepdims=True)
    p = jnp.exp(logits - m)
    afft_ref[...] = p / jnp.sum(p, axis=0, keepdims=True)


def _out_proj(x2, ya2, yb2, lw, tm):
    n, d = x2.shape
    row = lambda w: pl.BlockSpec((tm, w), lambda i: (i, 0))
    return pl.pallas_call(
        _out_proj_kernel,
        grid=(n // tm,),
        in_specs=[row(d), row(SWA_WIDTH), row(MLA_WIDTH),
                  _const_spec((1, SWA_WIDTH)), _const_spec((1, MLA_WIDTH)),
                  _const_spec((SWA_WIDTH, d)), _const_spec((MLA_WIDTH, d)),
                  _const_spec((1, d)), _const_spec((N_EXPERTS, d))],
        out_specs=(row(d), row(d // 2), pl.BlockSpec((N_EXPERTS, tm), lambda i: (0, i))),
        out_shape=(jax.ShapeDtypeStruct((n, d), F32), jax.ShapeDtypeStruct((n, d // 2), jnp.uint32),
                   jax.ShapeDtypeStruct((N_EXPERTS, n), F32)),
        compiler_params=_cparams("parallel"),
    )(x2, ya2, yb2, lw["g_ya"], lw["g_yb"], lw["w_out_a"], lw["w_out_b"], lw["g_ffn"], lw["w_router_t"])


SEL_CJ = 1024


def _token_cumsum(mask, upper, ones, lstrict):
    mb = mask.astype(BF16)
    within = _dot(mb, upper)
    tot = _dot(mb, ones)
    rowpre = _dot(lstrict, tot.astype(BF16))
    return within, tot, rowpre


def _select_kernel(a_ref, idx_ref, gate_ref, pos_ref, rowpre_ref, *, cap):
    a = a_ref[...]
    g = a.shape[0]
    bits = pltpu.bitcast(a, jnp.int32)

    def radix_step(i, t):
        cand = t | jnp.left_shift(jnp.int32(1), 30 - i)
        cnt = jnp.sum(jnp.where(bits >= cand, 1.0, 0.0), keepdims=True)
        return jnp.where(cnt >= cap, cand, t)

    thr = lax.fori_loop(0, 31, radix_step, jnp.zeros((1, 1), jnp.int32))
    ii = lax.broadcasted_iota(jnp.int32, (LANES, LANES), 0)
    jj = lax.broadcasted_iota(jnp.int32, (LANES, LANES), 1)
    upper = jnp.where(ii <= jj, 1.0, 0.0).astype(BF16)
    ones = jnp.ones((LANES, LANES), BF16)
    gi = lax.broadcasted_iota(jnp.int32, (g, g), 0)
    gj = lax.broadcasted_iota(jnp.int32, (g, g), 1)
    lstrict = jnp.where(gj < gi, 1.0, 0.0).astype(BF16)

    gt = bits > thr
    eq = bits == thr
    need = cap - jnp.sum(jnp.where(gt, 1.0, 0.0), keepdims=True)
    eqf = jnp.where(eq, 1.0, 0.0)
    w_eq, _, rp_eq = _token_cumsum(eqf, upper, ones, lstrict)
    tie_rank = w_eq - eqf + rp_eq
    self_ = jnp.where(gt, 1.0, jnp.where(tie_rank < need, eqf, 0.0))
    w, tot, rp = _token_cumsum(self_, upper, ones, lstrict)
    pos_ref[...] = jnp.where(self_ > 0.0, w - 1.0 + rp, -1.0)
    rpt = rp.T
    rowpre_ref[...] = rpt[0:1, :]

    rc = rp + tot
    wt = w.T.astype(BF16)
    pre_hi = jnp.floor(rpt[0:16] * (1.0 / 64.0))
    pre_lo = rpt[0:16] - 64.0 * pre_hi
    pre_hi = pre_hi.astype(BF16)
    pre_lo = pre_lo.astype(BF16)
    at = a.T
    a_h = at.astype(BF16)
    r1 = at - a_h.astype(F32)
    a_m = r1.astype(BF16)
    a_l = (r1 - a_m.astype(F32)).astype(BF16)
    cj_n = min(SEL_CJ, cap)
    for jc in range(cap // cj_n):
        j = (lax.broadcasted_iota(jnp.int32, (1, cj_n), 1) + jc * cj_n).astype(F32)
        rct = jnp.tile(rc, (1, cj_n // LANES))
        row = jnp.sum(jnp.where(rct <= j, 1.0, 0.0), axis=0, keepdims=True)
        grow = lax.broadcasted_iota(jnp.int32, (g, cj_n), 0).astype(F32)
        onehot = jnp.where(grow == row, 1.0, 0.0).astype(BF16)
        wg = _dot(wt, onehot)
        pre = 64.0 * _dot(pre_hi, onehot)[0:1] + _dot(pre_lo, onehot)[0:1]
        jloc = j - pre
        lane = jnp.sum(jnp.where(wg <= jloc, 1.0, 0.0), axis=0, keepdims=True)
        idx_ref[:, jc * cj_n:(jc + 1) * cj_n] = (row * LANES + lane).astype(jnp.int32)
        ag = _dot(a_h, onehot) + _dot(a_m, onehot) + _dot(a_l, onehot)
        lrow = lax.broadcasted_iota(jnp.int32, (LANES, cj_n), 0).astype(F32)
        gate_ref[:, jc * cj_n:(jc + 1) * cj_n] = jnp.sum(jnp.where(lrow == lane, ag, 0.0), axis=0,
                                                          keepdims=True)


def _select(afft, cap):
    e, n = afft.shape
    g = n // LANES
    blk = lambda w: pl.BlockSpec((None, 1, w), lambda i: (i, 0, 0))
    tok = pl.BlockSpec((None, g, LANES), lambda i: (i, 0, 0))
    return pl.pallas_call(
        functools.partial(_select_kernel, cap=cap),
        grid=(e,),
        in_specs=[tok],
        out_specs=(blk(cap), blk(cap), tok, blk(g)),
        out_shape=(jax.ShapeDtypeStruct((e, 1, cap), jnp.int32), jax.ShapeDtypeStruct((e, 1, cap), F32),
                   jax.ShapeDtypeStruct((e, g, LANES), F32), jax.ShapeDtypeStruct((e, 1, g), F32)),
        compiler_params=_cparams("parallel"),
    )(afft.reshape(e, g, LANES))


FFN_TT = 256
FFN_SLOTS = 3


def _ffn_kernel(idx_ref, h2_hbm, wgu_ref, wd_ref, gate_ref, ye_ref, xbuf, sem):
    tt = xbuf.shape[1]
    nj = pl.num_programs(1)
    step = pl.program_id(0) * nj + pl.program_id(1)
    last = pl.num_programs(0) * nj - 1
    nslot = xbuf.shape[0]
    ahead = nslot - 1
    slot = step % nslot

    def row_copy(tok, sl, r):
        return pltpu.make_async_copy(h2_hbm.at[pl.ds(tok, 1), :], xbuf.at[sl, pl.ds(r, 1), :], sem.at[sl])

    def issue(s):
        sl = s % nslot
        for r in range(tt):
            row_copy(idx_ref[s * tt + r], sl, r).start()

    def wait_slot(sl):
        pltpu.make_async_copy(h2_hbm.at[pl.ds(0, tt), :], xbuf.at[sl], sem.at[sl]).wait()

    @pl.when(step == 0)
    def _():
        for s in range(ahead):
            @pl.when(s <= last)
            def _():
                issue(s)

    @pl.when(step + ahead <= last)
    def _():
        issue(step + ahead)

    wait_slot(slot)
    xp = xbuf[slot]
    x = jnp.concatenate([pltpu.bitcast(xp & jnp.uint32(0xFFFF0000), F32).astype(BF16),
                         pltpu.bitcast(xp << 16, F32).astype(BF16)], axis=1)
    au = _dot(x, wgu_ref[...])
    a = au[:, :D_EXPERT]
    u = au[:, D_EXPERT:]
    act = (a / (1.0 + jnp.exp(-a)) * u).astype(BF16)
    ye_ref[...] = (_dot(act, wd_ref[...]) * gate_ref[...]).astype(BF16)


def _ffn(idx_flat, h2p, wgu, wd, layer, gate, cap):
    e, d = wgu.shape[1], wgu.shape[2]
    tt = _pick(cap, FFN_TT)
    return pl.pallas_call(
        _ffn_kernel,
        grid_spec=pltpu.PrefetchScalarGridSpec(
            num_scalar_prefetch=1,
            grid=(e, cap // tt),
            in_specs=[pl.BlockSpec(memory_space=pl.ANY),
                      pl.BlockSpec((None, None, d, 2 * D_EXPERT), lambda i, j, idx: (layer, i, 0, 0)),
                      pl.BlockSpec((None, None, D_EXPERT, d), lambda i, j, idx: (layer, i, 0, 0)),
                      pl.BlockSpec((None, tt, 1), lambda i, j, idx: (i, j, 0))],
            out_specs=pl.BlockSpec((None, tt, d), lambda i, j, idx: (i, j, 0)),
            scratch_shapes=[pltpu.VMEM((FFN_SLOTS, tt, d // 2), jnp.uint32),
                            pltpu.SemaphoreType.DMA((FFN_SLOTS,))]),
        out_shape=jax.ShapeDtypeStruct((e, cap, d), BF16),
        compiler_params=_cparams("arbitrary", "arbitrary"),
    )(idx_flat, h2p, wgu, wd, gate)


CAST_ROWS = 1024
BF16_SUBLANES = 16


def _row_block(n, limit):
    return max(r for r in range(BF16_SUBLANES, limit + 1, BF16_SUBLANES) if n % r == 0)


def _cast_gu_kernel(g_ref, u_ref, o_ref):
    o_ref[:, :D_EXPERT] = g_ref[...].astype(BF16)
    o_ref[:, D_EXPERT:] = u_ref[...].astype(BF16)


def _cast_kernel(w_ref, o_ref):
    o_ref[...] = w_ref[...].astype(BF16)


def _cast_experts(w_gate, w_up, w_down):
    l, e, d, f = w_gate.shape
    blk = lambda r, c: pl.BlockSpec((None, None, r, c), lambda a, b, i: (a, b, i, 0))
    rg = _row_block(d, CAST_ROWS // 2)
    wgu = pl.pallas_call(
        _cast_gu_kernel, grid=(l, e, d // rg),
        in_specs=[blk(rg, f), blk(rg, f)], out_specs=blk(rg, 2 * f),
        out_shape=jax.ShapeDtypeStruct((l, e, d, 2 * f), BF16),
        compiler_params=_cparams("parallel", "parallel", "parallel"),
    )(w_gate, w_up)
    rd = _row_block(f, CAST_ROWS)
    wd = pl.pallas_call(
        _cast_kernel, grid=(l, e, f // rd),
        in_specs=[blk(rd, d)], out_specs=blk(rd, d),
        out_shape=jax.ShapeDtypeStruct((l, e, f, d), BF16),
        compiler_params=_cparams("parallel", "parallel", "parallel"),
    )(w_down)
    return wgu, wd


CMB_TT = 128
CMB_BLK = 16
CMB_KC = 256
CMB_ROWS = N_EXPERTS * CMB_TT + N_EXPERTS * 2 * CMB_BLK + CMB_KC


def _combine_kernel(rowpre_ref, x_ref, pos_ref, ye_hbm, o_ref, buf, sem, acc_ref, *, g):
    i = pl.program_id(0)
    last = pl.num_programs(0) - 1
    slot = i % 2

    def plan(tile):
        starts, nblks, offs = [], [], []
        off = jnp.int32(0)
        for e in range(N_EXPERTS):
            lo = rowpre_ref[e * (g + 1) + tile]
            hi = rowpre_ref[e * (g + 1) + tile + 1]
            start = (lo >> 4) << 4
            nblk = jnp.where(hi > lo, (hi - start + (CMB_BLK - 1)) >> 4, 0)
            starts.append(start)
            nblks.append(nblk)
            offs.append(off)
            off = off + nblk
        return starts, nblks, offs, off

    def block_copy(e, src_row, dst_row, sl):
        return pltpu.make_async_copy(ye_hbm.at[e, pl.ds(src_row, CMB_BLK), :],
                                     buf.at[sl, pl.ds(dst_row, CMB_BLK), :], sem.at[sl])

    def issue(tile, sl):
        starts, nblks, offs, _ = plan(tile)
        for e in range(N_EXPERTS):
            def body(b, c, e=e):
                block_copy(e, pl.multiple_of(starts[e] + b * CMB_BLK, CMB_BLK),
                           pl.multiple_of((offs[e] + b) * CMB_BLK, CMB_BLK), sl).start()
                return c
            lax.fori_loop(0, nblks[e], body, 0)

    def wait(tile, sl):
        total = plan(tile)[3]

        def body(b, c):
            block_copy(0, 0, 0, sl).wait()
            return c
        lax.fori_loop(0, total, body, 0)

    @pl.when(i == 0)
    def _():
        buf[...] = jnp.zeros_like(buf)
        issue(0, 0)

    @pl.when(i < last)
    def _():
        issue(i + 1, 1 - slot)

    wait(i, slot)
    starts, nblks, offs, total = plan(i)
    pos = jnp.concatenate([pos_ref[...], jnp.zeros((LANES - N_EXPERTS, CMB_TT), F32)], axis=0).T
    cols = []
    for e in range(N_EXPERTS):
        pe = pos[:, e:e + 1]
        shift = (offs[e] * CMB_BLK - starts[e]).astype(F32)
        col = jnp.where(pe >= 0.0, pe + shift, -1.0)
        cols.append(jnp.broadcast_to(col, (CMB_TT, CMB_KC)))
    acc_ref[...] = x_ref[...]

    def chunk(k, c):
        k0 = pl.multiple_of(k * CMB_KC, CMB_KC)
        r = (lax.broadcasted_iota(jnp.int32, (CMB_TT, CMB_KC), 1) + k0).astype(F32)
        hit = jnp.where(cols[0] == r, 1.0, 0.0)
        for e in range(1, N_EXPERTS):
            hit = hit + jnp.where(cols[e] == r, 1.0, 0.0)
        acc_ref[...] += _dot(hit.astype(BF16), buf[slot, pl.ds(k0, CMB_KC), :])
        return c

    lax.fori_loop(0, (total * CMB_BLK + (CMB_KC - 1)) // CMB_KC, chunk, 0)
    o_ref[...] = acc_ref[...]


def _combine(rowpre_flat, x2, pos, ye, g):
    n, d = x2.shape
    assert n == g * CMB_TT
    return pl.pallas_call(
        functools.partial(_combine_kernel, g=g),
        grid_spec=pltpu.PrefetchScalarGridSpec(
            num_scalar_prefetch=1,
            grid=(g,),
            in_specs=[pl.BlockSpec((CMB_TT, d), lambda i, rp: (i, 0)),
                      pl.BlockSpec((N_EXPERTS, CMB_TT), lambda i, rp: (0, i)),
                      pl.BlockSpec(memory_space=pl.ANY)],
            out_specs=pl.BlockSpec((CMB_TT, d), lambda i, rp: (i, 0)),
            scratch_shapes=[pltpu.VMEM((2, CMB_ROWS, d), BF16), pltpu.SemaphoreType.DMA((2,)),
                            pltpu.VMEM((CMB_TT, d), F32)]),
        out_shape=jax.ShapeDtypeStruct((n, d), F32),
        compiler_params=_cparams("arbitrary"),
    )(rowpre_flat, x2, pos, ye)


def _final_norm_kernel(x_ref, g_ref, o_ref):
    o_ref[...] = _rms(x_ref[...], g_ref[...])


def _final_norm(x2, g, tm):
    n, d = x2.shape
    return pl.pallas_call(
        _final_norm_kernel,
        grid=(n // tm,),
        in_specs=[pl.BlockSpec((tm, d), lambda i: (i, 0)), _const_spec((1, d))],
        out_specs=pl.BlockSpec((tm, d), lambda i: (i, 0)),
        out_shape=jax.ShapeDtypeStruct((n, d), F32),
        compiler_params=_cparams("parallel"),
    )(x2, g)


def _rot_half_cols(w):
    half = w.shape[-1] // 2
    return jnp.concatenate([-w[..., half:], w[..., :half]], axis=-1)


def _prep_layer(l, g_attn, w_in, sink, g_cq, w_uq, g_ckv, w_uk, w_uv, g_ya, g_yb, w_out, g_ffn,
                w_router):
    wi = w_in[l]
    w_kr = wi[:, SRC_KR:SRC_KR + QK_ROPE]
    zpad = jnp.zeros((D_MODEL, 128 - QK_ROPE), F32)
    w_in_ext = jnp.concatenate([wi[:, SRC_KA:SRC_VA], wi[:, SRC_CQ:SRC_KR], w_kr, zpad,
                                _rot_half_cols(w_kr), zpad], axis=1)
    w_in_t = jnp.concatenate([wi[:, SRC_QA:SRC_KA], wi[:, SRC_VA:SRC_CQ]], axis=1).T
    wq = w_uq[l].reshape(Q_LORA, MLA_HEADS, QK_NOPE + QK_ROPE)
    wq_rope = wq[:, :, QK_NOPE:]
    zq = jnp.zeros((Q_LORA, MLA_HEADS, 128 - QK_ROPE), F32)
    w_uq_main = jnp.concatenate([wq, zq], axis=-1).reshape(Q_LORA, MLA_HEADS * MLA_QK)
    w_uq_rot = jnp.concatenate([_rot_half_cols(wq_rope), zq], axis=-1).reshape(Q_LORA, MLA_HEADS * 128)
    return dict(
        g_attn=g_attn[l][None], w_in=w_in_ext.astype(BF16), w_in_t=w_in_t.astype(BF16), sink=sink[l],
        g_cq=g_cq[l][None], w_uq_main_t=w_uq_main.T.astype(BF16), w_uq_rot_t=w_uq_rot.T.astype(BF16),
        g_ckv=g_ckv[l][None], w_uk=w_uk[l].astype(BF16), w_uvt=w_uv[l].T.astype(BF16),
        g_ya=g_ya[l][None], g_yb=g_yb[l][None],
        w_out_a=w_out[l][:SWA_WIDTH].astype(BF16), w_out_b=w_out[l][SWA_WIDTH:].astype(BF16),
        g_ffn=g_ffn[l][None], w_router_t=w_router[l].T.astype(BF16), layer=l,
    )


def _rope_tables(s):
    inv = 1.0 / (ROPE_THETA ** (jnp.arange(0, QK_ROPE, 2, dtype=F32) / QK_ROPE))
    ang = jnp.arange(s, dtype=F32)[:, None] * inv[None, :]
    z = jnp.zeros((s, 128 - QK_ROPE), F32)
    cos, sin = jnp.cos(ang), jnp.sin(ang)
    cos128 = jnp.concatenate([cos, cos, z], axis=1)
    sin128 = jnp.concatenate([sin, sin, z], axis=1)
    return cos128, sin128, cos128.T, sin128.T


def _pick(n, pref):
    t = pref
    while n % t:
        t //= 2
    return t


def _layer(x, lw, experts, tables):
    b, s, d = x.shape
    n = b * s
    qat, ka, vat, qmt, km, vt = _in_proj(x, lw, tables, _pick(s, MLA_TK))
    ya = _swa(qat, ka, vat, lw["sink"])
    yb = _mla(qmt, km, vt)
    x2, h2, afft = _out_proj(x.reshape(n, d), ya.reshape(n, SWA_WIDTH), yb.reshape(n, MLA_WIDTH), lw,
                             _pick(n, 512))
    cap = CAPACITY_FACTOR * n // N_EXPERTS
    g = n // LANES
    idx, gate, pos, rowpre = _select(afft, cap)
    ye = _ffn(idx.reshape(-1), h2, *experts, lw["layer"], gate.reshape(N_EXPERTS, cap, 1), cap)
    bounds = jnp.concatenate([rowpre.reshape(N_EXPERTS, g), jnp.full((N_EXPERTS, 1), cap, F32)], axis=1)
    x2 = _combine(bounds.astype(jnp.int32).reshape(-1), x2, pos.reshape(N_EXPERTS, n), ye, g)
    return x2.reshape(b, s, d)


def _trunk(x, layers, experts, g_final):
    b, s, d = x.shape
    tables = _rope_tables(s)
    for lw in layers:
        x = _layer(x, lw, experts, tables)
    return _final_norm(x.reshape(b * s, d), g_final[None], _pick(b * s, 512)).reshape(b, s, d)


def kernel(x_prompt, x_sample, g_attn, w_in, sink, g_cq, w_uq, g_ckv, w_uk, w_uv, g_ya, g_yb, w_out,
           g_ffn, w_router, w_gate, w_up, w_down, g_final):
    depth = w_in.shape[0]
    layers = [_prep_layer(l, g_attn, w_in, sink, g_cq, w_uq, g_ckv, w_uk, w_uv, g_ya, g_yb, w_out,
                          g_ffn, w_router) for l in range(depth)]
    experts = _cast_experts(w_gate, w_up, w_down)
    return _trunk(x_prompt, layers, experts, g_final), _trunk(x_sample, layers, experts, g_final)
```

```python
import functools

import numpy as np
import jax
import jax.numpy as jnp
from jax import lax
from jax.experimental import pallas as pl
from jax.experimental.pallas import tpu as pltpu

F32 = jnp.float32
BF16 = jnp.bfloat16

D_MODEL = 2048
HEAD_DIM = 128
SWA_HEADS = 8
SWA_KV_HEADS = 2
SWA_GROUP = SWA_HEADS // SWA_KV_HEADS
WINDOW = 128
BLOCK = 128
MLA_HEADS = 8
QK_NOPE = 128
QK_ROPE = 64
V_DIM = 128
Q_LORA = 512
KV_LORA = 512
ROPE_THETA = 10000.0
SWA_WIDTH = SWA_HEADS * HEAD_DIM
SWA_KV_WIDTH = SWA_KV_HEADS * HEAD_DIM
MLA_WIDTH = MLA_HEADS * V_DIM
N_EXPERTS = 16
D_EXPERT = 1408
CAPACITY_FACTOR = 2
EPS = 1e-6
LANES = 128

SRC_QA = 0
SRC_KA = SRC_QA + SWA_WIDTH
SRC_VA = SRC_KA + SWA_KV_WIDTH
SRC_CQ = SRC_VA + SWA_KV_WIDTH
SRC_KR = SRC_CQ + Q_LORA + KV_LORA
OFF_KA = 0
OFF_CQ = OFF_KA + SWA_KV_WIDTH
OFF_CKV = OFF_CQ + Q_LORA
OFF_KR = OFF_CKV + KV_LORA
OFF_KRR = OFF_KR + 128
IN_EXT = OFF_KRR + 128
TOFF_QA = 0
TOFF_VA = TOFF_QA + SWA_WIDTH
IN_T = TOFF_VA + SWA_KV_WIDTH
MLA_QK = 256

SWA_SCALE = HEAD_DIM ** -0.5
MLA_SCALE = (QK_NOPE + QK_ROPE) ** -0.5 * float(np.log2(np.e))
NEG_BIG = -1e30
MLA_TQ = 512
MLA_TK = 512

V7X_VMEM_BYTES = 64 * 1024 * 1024
VMEM_LIMIT = V7X_VMEM_BYTES - 8 * 1024 * 1024


def _cparams(*sem):
    return pltpu.CompilerParams(dimension_semantics=sem, vmem_limit_bytes=VMEM_LIMIT)


def _const_spec(shape):
    nd = len(shape)
    return pl.BlockSpec(shape, lambda *_: (0,) * nd, pipeline_mode=pl.Buffered(1))


def _rms(x, g):
    return x * lax.rsqrt(jnp.mean(x * x, axis=-1, keepdims=True) + EPS) * g


def _dot(a, b):
    return jnp.dot(a, b, preferred_element_type=F32)


def _dot_nt(a, b):
    return lax.dot_general(a, b, (((1,), (1,)), ((), ())), preferred_element_type=F32)


def _in_proj_kernel(x_ref, g_ref, win_ref, wint_ref, gcq_ref, wqmt_ref, wqrt_ref, gckv_ref, wuk_ref,
                    wuvt_ref, cos_ref, sin_ref, cost_ref, sint_ref,
                    qat_ref, ka_ref, vat_ref, qmt_ref, km_ref, vt_ref):
    h = _rms(x_ref[...], g_ref[...]).astype(BF16)
    proj = _dot(h, win_ref[...])
    proj_t = _dot_nt(wint_ref[...], h)
    qat_ref[...] = (proj_t[TOFF_QA:TOFF_VA] * SWA_SCALE).astype(BF16)
    vat_ref[...] = proj_t[TOFF_VA:IN_T].astype(BF16)
    ka_ref[...] = proj[:, OFF_KA:OFF_CQ].astype(BF16)
    cos = cos_ref[...]
    sin = sin_ref[...]
    krope = (proj[:, OFF_KR:OFF_KRR] * cos + proj[:, OFF_KRR:IN_EXT] * sin).astype(BF16)
    cqn = _rms(proj[:, OFF_CQ:OFF_CKV], gcq_ref[...]).astype(BF16)
    qmain = _dot_nt(wqmt_ref[...], cqn)
    qrot = _dot_nt(wqrt_ref[...], cqn)
    cos_t = cost_ref[...]
    sin_t = sint_ref[...]
    for hh in range(MLA_HEADS):
        lo = hh * MLA_QK
        qmt_ref[lo:lo + QK_NOPE, :] = (qmain[lo:lo + QK_NOPE] * MLA_SCALE).astype(BF16)
        rp = qmain[lo + QK_NOPE:lo + MLA_QK] * cos_t + qrot[hh * 128:(hh + 1) * 128] * sin_t
        qmt_ref[lo + QK_NOPE:lo + MLA_QK, :] = (rp * MLA_SCALE).astype(BF16)
    ckvn = _rms(proj[:, OFF_CKV:OFF_KR], gckv_ref[...]).astype(BF16)
    kn = _dot(ckvn, wuk_ref[...])
    for hh in range(MLA_HEADS):
        lo = hh * MLA_QK
        km_ref[:, lo:lo + QK_NOPE] = kn[:, hh * QK_NOPE:(hh + 1) * QK_NOPE].astype(BF16)
        km_ref[:, lo + QK_NOPE:lo + MLA_QK] = krope
    vt_ref[...] = _dot_nt(wuvt_ref[...], ckvn).astype(BF16)


def _in_proj(x, lw, tables, tm):
    b, s, d = x.shape
    row = lambda w: pl.BlockSpec((None, tm, w), lambda i, j: (i, j, 0))
    tab = pl.BlockSpec((tm, 128), lambda i, j: (j, 0))
    tab_t = pl.BlockSpec((128, tm), lambda i, j: (0, j))
    col = lambda w: pl.BlockSpec((None, w, tm), lambda i, j: (i, 0, j))
    out_shapes = (
        jax.ShapeDtypeStruct((b, SWA_WIDTH, s), BF16),
        jax.ShapeDtypeStruct((b, s, SWA_KV_WIDTH), BF16),
        jax.ShapeDtypeStruct((b, SWA_KV_WIDTH, s), BF16),
        jax.ShapeDtypeStruct((b, MLA_HEADS * MLA_QK, s), BF16),
        jax.ShapeDtypeStruct((b, s, MLA_HEADS * MLA_QK), BF16),
        jax.ShapeDtypeStruct((b, s // tm, MLA_WIDTH, tm), BF16),
    )
    return pl.pallas_call(
        _in_proj_kernel,
        grid=(b, s // tm),
        in_specs=[
            row(d),
            _const_spec((1, d)),
            _const_spec((d, IN_EXT)),
            _const_spec((IN_T, d)),
            _const_spec((1, Q_LORA)),
            _const_spec((MLA_HEADS * MLA_QK, Q_LORA)),
            _const_spec((MLA_HEADS * 128, Q_LORA)),
            _const_spec((1, KV_LORA)),
            _const_spec((KV_LORA, MLA_HEADS * QK_NOPE)),
            _const_spec((MLA_WIDTH, KV_LORA)),
            tab, tab, tab_t, tab_t,
        ],
        out_specs=(col(SWA_WIDTH), row(SWA_KV_WIDTH), col(SWA_KV_WIDTH), col(MLA_HEADS * MLA_QK),
                   row(MLA_HEADS * MLA_QK),
                   pl.BlockSpec((None, None, MLA_WIDTH, tm), lambda i, j: (i, j, 0, 0))),
        out_shape=out_shapes,
        compiler_params=_cparams("parallel", "parallel"),
    )(x, lw["g_attn"], lw["w_in"], lw["w_in_t"], lw["g_cq"], lw["w_uq_main_t"], lw["w_uq_rot_t"], lw["g_ckv"],
      lw["w_uk"], lw["w_uvt"], *tables)


def _alibi_slopes():
    return [float(np.float32(2.0 ** (-8.0 * (i + 1) / SWA_HEADS))) for i in range(SWA_HEADS)]


SWA_TQ = 512


def _swa_kernel(sink_ref, qt_ref, kp_ref, kc_ref, kn_ref, vtp_ref, vtc_ref, vtn_ref, o_ref, *, seq):
    tq = qt_ref.shape[1]
    base = pl.program_id(1) * tq
    ki = lax.broadcasted_iota(jnp.int32, (3 * BLOCK, BLOCK), 0)
    qj = lax.broadcasted_iota(jnp.int32, (3 * BLOCK, BLOCK), 1)
    rel = ki - BLOCK - qj
    in_band = jnp.abs(rel) <= WINDOW
    dist = jnp.abs(rel).astype(F32)
    slopes = _alibi_slopes()
    kall = jnp.concatenate([kp_ref[...], kc_ref[...], kn_ref[...]], axis=0)
    vtall = jnp.concatenate([vtp_ref[...], vtc_ref[...], vtn_ref[...]], axis=1)
    for sb in range(tq // BLOCK):
        spos = base + sb * BLOCK + ki - BLOCK
        valid = in_band & (spos >= 0) & (spos < seq)
        cols = slice(sb * BLOCK, (sb + 1) * BLOCK)
        win = slice(sb * BLOCK, (sb + 3) * BLOCK)
        for kvh in range(SWA_KV_HEADS):
            kh = kall[win, kvh * HEAD_DIM:(kvh + 1) * HEAD_DIM]
            vth = vtall[kvh * HEAD_DIM:(kvh + 1) * HEAD_DIM, win]
            heads = [kvh * SWA_GROUP + g for g in range(SWA_GROUP)]
            qt = jnp.concatenate([qt_ref[hd * HEAD_DIM:(hd + 1) * HEAD_DIM, cols] for hd in heads], axis=1)
            raw = _dot(kh, qt)
            logits = jnp.concatenate(
                [jnp.where(valid, raw[:, g * BLOCK:(g + 1) * BLOCK] - slopes[hd] * dist, NEG_BIG)
                 for g, hd in enumerate(heads)], axis=1)
            sk = jnp.concatenate([jnp.full((1, BLOCK), sink_ref[hd], F32) for hd in heads], axis=1)
            m = jnp.maximum(jnp.max(logits, axis=0, keepdims=True), sk)
            p = jnp.exp(logits - m)
            denom = jnp.sum(p, axis=0, keepdims=True) + jnp.exp(sk - m)
            ot = _dot(vth, p.astype(BF16)) / denom
            for g, hd in enumerate(heads):
                o_ref[cols, hd * HEAD_DIM:(hd + 1) * HEAD_DIM] = ot[:, g * BLOCK:(g + 1) * BLOCK].T.astype(BF16)


def _swa(qat, ka, vat, sink):
    b, s, _ = ka.shape
    tq = _pick(s, SWA_TQ)
    r = tq // BLOCK
    nb = s // BLOCK
    prev = lambda j: jnp.maximum(j * r - 1, 0)
    nxt = lambda j: jnp.minimum(j * r + r, nb - 1)
    kedge = lambda f: pl.BlockSpec((None, BLOCK, SWA_KV_WIDTH), lambda i, j: (i, f(j), 0))
    vedge = lambda f: pl.BlockSpec((None, SWA_KV_WIDTH, BLOCK), lambda i, j: (i, 0, f(j)))
    return pl.pallas_call(
        functools.partial(_swa_kernel, seq=s),
        grid=(b, s // tq),
        in_specs=[pl.BlockSpec(memory_space=pltpu.SMEM),
                  pl.BlockSpec((None, SWA_WIDTH, tq), lambda i, j: (i, 0, j)),
                  kedge(prev), pl.BlockSpec((None, tq, SWA_KV_WIDTH), lambda i, j: (i, j, 0)), kedge(nxt),
                  vedge(prev), pl.BlockSpec((None, SWA_KV_WIDTH, tq), lambda i, j: (i, 0, j)), vedge(nxt)],
        out_specs=pl.BlockSpec((None, tq, SWA_WIDTH), lambda i, j: (i, j, 0)),
        out_shape=jax.ShapeDtypeStruct((b, s, SWA_WIDTH), BF16),
        compiler_params=_cparams("parallel", "parallel"),
    )(sink, qat, ka, ka, ka, vat, vat, vat)


MLA_NSUB = 2
MLA_SUM_ROWS = 16


def _mla_kernel(q_ref, k_ref, vt_ref, o_ref, st_ref):
    nsub, _, tk, tq = st_ref.shape
    nchunk = vt_ref.shape[0]
    assert nchunk % 2 == 0
    qs = [q_ref[:, t * tq:(t + 1) * tq] for t in range(nsub)]

    def produce(t, slot, c):
        k0 = pl.multiple_of(c * tk, tk)
        st = _dot(k_ref[pl.ds(k0, tk), :], qs[t])
        st_ref[t, slot] = st
        return jnp.max(st, axis=0, keepdims=True)

    ones_rows = jnp.ones((MLA_SUM_ROWS, tk), BF16)

    def consume(t, slot, c, mx, state):
        m, acc = state
        m_new = jnp.maximum(m, mx)
        alpha = jnp.exp2(m - m_new)
        p = jnp.exp2(st_ref[t, slot] - m_new)
        vext = jnp.concatenate([vt_ref[c], ones_rows], axis=0)
        acc = alpha * acc + _dot(vext, p.astype(BF16))
        return m_new, acc

    def pair(c, mx0, states, produce_next):
        mx1 = [produce(t, 1, c + 1) for t in range(nsub)]
        states = [consume(t, 0, c, mx0[t], states[t]) for t in range(nsub)]
        if produce_next:
            mx0 = [produce(t, 0, c + 2) for t in range(nsub)]
        states = [consume(t, 1, c + 1, mx1[t], states[t]) for t in range(nsub)]
        return mx0, states

    def body(i, carry):
        return pair(2 * i, *carry, produce_next=True)

    state0 = (jnp.full((1, tq), NEG_BIG, F32), jnp.zeros((V_DIM + MLA_SUM_ROWS, tq), F32))
    init = ([produce(t, 0, 0) for t in range(nsub)], [state0] * nsub)
    mx0, states = lax.fori_loop(0, nchunk // 2 - 1, body, init, unroll=2)
    _, states = pair(nchunk - 2, mx0, states, produce_next=False)
    for t in range(nsub):
        acc = states[t][1]
        o_ref[t * tq:(t + 1) * tq, :] = (acc[:V_DIM] / acc[V_DIM:V_DIM + 1]).T.astype(BF16)


def _mla(qmt, km, vt):
    b, s, _ = km.shape
    nchunk, tk = vt.shape[1], vt.shape[3]
    tq = _pick(s, MLA_TQ)
    nsub = MLA_NSUB if s % (MLA_NSUB * tq) == 0 else 1
    return pl.pallas_call(
        _mla_kernel,
        grid=(b, MLA_HEADS, s // (nsub * tq)),
        in_specs=[pl.BlockSpec((None, MLA_QK, nsub * tq), lambda bb, h, i: (bb, h, i)),
                  pl.BlockSpec((None, s, MLA_QK), lambda bb, h, i: (bb, 0, h)),
                  pl.BlockSpec((None, nchunk, V_DIM, tk), lambda bb, h, i: (bb, 0, h, 0))],
        out_specs=pl.BlockSpec((None, nsub * tq, V_DIM), lambda bb, h, i: (bb, i, h)),
        out_shape=jax.ShapeDtypeStruct((b, s, MLA_WIDTH), BF16),
        scratch_shapes=[pltpu.VMEM((nsub, 2, tk, tq), F32)],
        compiler_params=_cparams("parallel", "parallel", "arbitrary"),
    )(qmt, km, vt)


def _out_proj_kernel(x_ref, ya_ref, yb_ref, gya_ref, gyb_ref, woa_ref, wob_ref, gffn_ref, wrt_ref,
                     xo_ref, h2_ref, afft_ref):
    yan = _rms(ya_ref[...].astype(F32), gya_ref[...]).astype(BF16)
    ybn = _rms(yb_ref[...].astype(F32), gyb_ref[...]).astype(BF16)
    xn = x_ref[...] + _dot(yan, woa_ref[...]) + _dot(ybn, wob_ref[...])
    xo_ref[...] = xn
    h2 = _rms(xn, gffn_ref[...]).astype(BF16)
    half = h2.shape[1] // 2
    hi = pltpu.bitcast(h2[:, :half].astype(F32), jnp.uint32)
    lo = pltpu.bitcast(h2[:, half:].astype(F32), jnp.uint32)
    h2_ref[...] = hi | (lo >> 16)
    logits = _dot_nt(wrt_ref[...], h2)
    m = jnp.max(logits, axis=0, keepdims=True)
    p = jnp.exp(logits - m)
    afft_ref[...] = p / jnp.sum(p, axis=0, keepdims=True)


def _out_proj(x2, ya2, yb2, lw, tm):
    n, d = x2.shape
    row = lambda w: pl.BlockSpec((tm, w), lambda i: (i, 0))
    return pl.pallas_call(
        _out_proj_kernel,
        grid=(n // tm,),
        in_specs=[row(d), row(SWA_WIDTH), row(MLA_WIDTH),
                  _const_spec((1, SWA_WIDTH)), _const_spec((1, MLA_WIDTH)),
                  _const_spec((SWA_WIDTH, d)), _const_spec((MLA_WIDTH, d)),
                  _const_spec((1, d)), _const_spec((N_EXPERTS, d))],
        out_specs=(row(d), row(d // 2), pl.BlockSpec((N_EXPERTS, tm), lambda i: (0, i))),
        out_shape=(jax.ShapeDtypeStruct((n, d), F32), jax.ShapeDtypeStruct((n, d // 2), jnp.uint32),
                   jax.ShapeDtypeStruct((N_EXPERTS, n), F32)),
        compiler_params=_cparams("parallel"),
    )(x2, ya2, yb2, lw["g_ya"], lw["g_yb"], lw["w_out_a"], lw["w_out_b"], lw["g_ffn"], lw["w_router_t"])


SEL_CJ = 1024


def _token_cumsum(mask, upper, ones, lstrict):
    mb = mask.astype(BF16)
    within = _dot(mb, upper)
    tot = _dot(mb, ones)
    rowpre = _dot(lstrict, tot.astype(BF16))
    return within, tot, rowpre


def _select_kernel(a_ref, idx_ref, gate_ref, pos_ref, rowpre_ref, *, cap):
    a = a_ref[...]
    g = a.shape[0]
    bits = pltpu.bitcast(a, jnp.int32)

    def radix_step(i, t):
        cand = t | jnp.left_shift(jnp.int32(1), 30 - i)
        cnt = jnp.sum(jnp.where(bits >= cand, 1.0, 0.0), keepdims=True)
        return jnp.where(cnt >= cap, cand, t)

    thr = lax.fori_loop(0, 31, radix_step, jnp.zeros((1, 1), jnp.int32))
    ii = lax.broadcasted_iota(jnp.int32, (LANES, LANES), 0)
    jj = lax.broadcasted_iota(jnp.int32, (LANES, LANES), 1)
    upper = jnp.where(ii <= jj, 1.0, 0.0).astype(BF16)
    ones = jnp.ones((LANES, LANES), BF16)
    gi = lax.broadcasted_iota(jnp.int32, (g, g), 0)
    gj = lax.broadcasted_iota(jnp.int32, (g, g), 1)
    lstrict = jnp.where(gj < gi, 1.0, 0.0).astype(BF16)

    gt = bits > thr
    eq = bits == thr
    need = cap - jnp.sum(jnp.where(gt, 1.0, 0.0), keepdims=True)
    eqf = jnp.where(eq, 1.0, 0.0)
    w_eq, _, rp_eq = _token_cumsum(eqf, upper, ones, lstrict)
    tie_rank = w_eq - eqf + rp_eq
    self_ = jnp.where(gt, 1.0, jnp.where(tie_rank < need, eqf, 0.0))
    w, tot, rp = _token_cumsum(self_, upper, ones, lstrict)
    pos_ref[...] = jnp.where(self_ > 0.0, w - 1.0 + rp, -1.0)
    rpt = rp.T
    rowpre_ref[...] = rpt[0:1, :]

    rc = rp + tot
    wt = w.T.astype(BF16)
    pre_hi = jnp.floor(rpt[0:16] * (1.0 / 64.0))
    pre_lo = rpt[0:16] - 64.0 * pre_hi
    pre_hi = pre_hi.astype(BF16)
    pre_lo = pre_lo.astype(BF16)
    at = a.T
    a_h = at.astype(BF16)
    r1 = at - a_h.astype(F32)
    a_m = r1.astype(BF16)
    a_l = (r1 - a_m.astype(F32)).astype(BF16)
    cj_n = min(SEL_CJ, cap)
    for jc in range(cap // cj_n):
        j = (lax.broadcasted_iota(jnp.int32, (1, cj_n), 1) + jc * cj_n).astype(F32)
        rct = jnp.tile(rc, (1, cj_n // LANES))
        row = jnp.sum(jnp.where(rct <= j, 1.0, 0.0), axis=0, keepdims=True)
        grow = lax.broadcasted_iota(jnp.int32, (g, cj_n), 0).astype(F32)
        onehot = jnp.where(grow == row, 1.0, 0.0).astype(BF16)
        wg = _dot(wt, onehot)
        pre = 64.0 * _dot(pre_hi, onehot)[0:1] + _dot(pre_lo, onehot)[0:1]
        jloc = j - pre
        lane = jnp.sum(jnp.where(wg <= jloc, 1.0, 0.0), axis=0, keepdims=True)
        idx_ref[:, jc * cj_n:(jc + 1) * cj_n] = (row * LANES + lane).astype(jnp.int32)
        ag = _dot(a_h, onehot) + _dot(a_m, onehot) + _dot(a_l, onehot)
        lrow = lax.broadcasted_iota(jnp.int32, (LANES, cj_n), 0).astype(F32)
        gate_ref[:, jc * cj_n:(jc + 1) * cj_n] = jnp.sum(jnp.where(lrow == lane, ag, 0.0), axis=0,
                                                          keepdims=True)


def _select(afft, cap):
    e, n = afft.shape
    g = n // LANES
    blk = lambda w: pl.BlockSpec((None, 1, w), lambda i: (i, 0, 0))
    tok = pl.BlockSpec((None, g, LANES), lambda i: (i, 0, 0))
    return pl.pallas_call(
        functools.partial(_select_kernel, cap=cap),
        grid=(e,),
        in_specs=[tok],
        out_specs=(blk(cap), blk(cap), tok, blk(g)),
        out_shape=(jax.ShapeDtypeStruct((e, 1, cap), jnp.int32), jax.ShapeDtypeStruct((e, 1, cap), F32),
                   jax.ShapeDtypeStruct((e, g, LANES), F32), jax.ShapeDtypeStruct((e, 1, g), F32)),
        compiler_params=_cparams("parallel"),
    )(afft.reshape(e, g, LANES))


FFN_TT = 256
FFN_SPLIT = 2
FFN_SLOTS = 3


def _ffn_kernel(idx_ref, h2_hbm, wgu_ref, wd_ref, gate_ref, ye_ref, xbuf, sem):
    tt = xbuf.shape[1]
    nj = pl.num_programs(1)
    step = pl.program_id(0) * nj + pl.program_id(1)
    last = pl.num_programs(0) * nj - 1
    nslot = xbuf.shape[0]
    ahead = nslot - 1
    slot = step % nslot

    def row_copy(tok, sl, r):
        return pltpu.make_async_copy(h2_hbm.at[pl.ds(tok, 1), :], xbuf.at[sl, pl.ds(r, 1), :], sem.at[sl])

    def issue(s):
        sl = s % nslot
        for r in range(tt):
            row_copy(idx_ref[s * tt + r], sl, r).start()

    def wait_slot(sl):
        pltpu.make_async_copy(h2_hbm.at[pl.ds(0, tt), :], xbuf.at[sl], sem.at[sl]).wait()

    @pl.when(step == 0)
    def _():
        for s in range(ahead):
            @pl.when(s <= last)
            def _():
                issue(s)

    @pl.when(step + ahead <= last)
    def _():
        issue(step + ahead)

    wait_slot(slot)
    xp = xbuf[slot]
    x = jnp.concatenate([pltpu.bitcast(xp & jnp.uint32(0xFFFF0000), F32).astype(BF16),
                         pltpu.bitcast(xp << 16, F32).astype(BF16)], axis=1)
    for part in range(FFN_SPLIT):
        r = slice(part * (tt // FFN_SPLIT), (part + 1) * (tt // FFN_SPLIT))
        au = _dot(x[r], wgu_ref[...])
        a = au[:, :D_EXPERT]
        u = au[:, D_EXPERT:]
        act = (a / (1.0 + jnp.exp(-a)) * u).astype(BF16)
        ye_ref[r, :] = (_dot(act, wd_ref[...]) * gate_ref[r, :]).astype(BF16)


def _ffn(idx_flat, h2p, wgu, wd, layer, gate, cap):
    e, d = wgu.shape[1], wgu.shape[2]
    tt = _pick(cap, FFN_TT)
    return pl.pallas_call(
        _ffn_kernel,
        grid_spec=pltpu.PrefetchScalarGridSpec(
            num_scalar_prefetch=1,
            grid=(e, cap // tt),
            in_specs=[pl.BlockSpec(memory_space=pl.ANY),
                      pl.BlockSpec((None, None, d, 2 * D_EXPERT), lambda i, j, idx: (layer, i, 0, 0)),
                      pl.BlockSpec((None, None, D_EXPERT, d), lambda i, j, idx: (layer, i, 0, 0)),
                      pl.BlockSpec((None, tt, 1), lambda i, j, idx: (i, j, 0))],
            out_specs=pl.BlockSpec((None, tt, d), lambda i, j, idx: (i, j, 0)),
            scratch_shapes=[pltpu.VMEM((FFN_SLOTS, tt, d // 2), jnp.uint32),
                            pltpu.SemaphoreType.DMA((FFN_SLOTS,))]),
        out_shape=jax.ShapeDtypeStruct((e, cap, d), BF16),
        compiler_params=_cparams("arbitrary", "arbitrary"),
    )(idx_flat, h2p, wgu, wd, gate)


CAST_ROWS = 1024
BF16_SUBLANES = 16


def _row_block(n, limit):
    return max(r for r in range(BF16_SUBLANES, limit + 1, BF16_SUBLANES) if n % r == 0)


def _cast_gu_kernel(g_ref, u_ref, o_ref):
    o_ref[:, :D_EXPERT] = g_ref[...].astype(BF16)
    o_ref[:, D_EXPERT:] = u_ref[...].astype(BF16)


def _cast_kernel(w_ref, o_ref):
    o_ref[...] = w_ref[...].astype(BF16)


def _cast_experts(w_gate, w_up, w_down):
    l, e, d, f = w_gate.shape
    blk = lambda r, c: pl.BlockSpec((None, None, r, c), lambda a, b, i: (a, b, i, 0))
    rg = _row_block(d, CAST_ROWS // 2)
    wgu = pl.pallas_call(
        _cast_gu_kernel, grid=(l, e, d // rg),
        in_specs=[blk(rg, f), blk(rg, f)], out_specs=blk(rg, 2 * f),
        out_shape=jax.ShapeDtypeStruct((l, e, d, 2 * f), BF16),
        compiler_params=_cparams("parallel", "parallel", "parallel"),
    )(w_gate, w_up)
    rd = _row_block(f, CAST_ROWS)
    wd = pl.pallas_call(
        _cast_kernel, grid=(l, e, f // rd),
        in_specs=[blk(rd, d)], out_specs=blk(rd, d),
        out_shape=jax.ShapeDtypeStruct((l, e, f, d), BF16),
        compiler_params=_cparams("parallel", "parallel", "parallel"),
    )(w_down)
    return wgu, wd


CMB_TT = 128
CMB_BLK = 16
CMB_KC = 256
CMB_ROWS = N_EXPERTS * CMB_TT + N_EXPERTS * 2 * CMB_BLK + CMB_KC


def _combine_kernel(rowpre_ref, x_ref, pos_ref, gfin_ref, ye_hbm, o_ref, buf, sem, acc_ref, *, g, final):
    i = pl.program_id(0)
    last = pl.num_programs(0) - 1
    slot = i % 2

    def plan(tile):
        starts, nblks, offs = [], [], []
        off = jnp.int32(0)
        for e in range(N_EXPERTS):
            lo = rowpre_ref[e * (g + 1) + tile]
            hi = rowpre_ref[e * (g + 1) + tile + 1]
            start = (lo >> 4) << 4
            nblk = jnp.where(hi > lo, (hi - start + (CMB_BLK - 1)) >> 4, 0)
            starts.append(start)
            nblks.append(nblk)
            offs.append(off)
            off = off + nblk
        return starts, nblks, offs, off

    def block_copy(e, src_row, dst_row, sl):
        return pltpu.make_async_copy(ye_hbm.at[e, pl.ds(src_row, CMB_BLK), :],
                                     buf.at[sl, pl.ds(dst_row, CMB_BLK), :], sem.at[sl])

    def issue(tile, sl):
        starts, nblks, offs, _ = plan(tile)
        for e in range(N_EXPERTS):
            def body(b, c, e=e):
                block_copy(e, pl.multiple_of(starts[e] + b * CMB_BLK, CMB_BLK),
                           pl.multiple_of((offs[e] + b) * CMB_BLK, CMB_BLK), sl).start()
                return c
            lax.fori_loop(0, nblks[e], body, 0)

    def wait(tile, sl):
        total = plan(tile)[3]

        def body(b, c):
            block_copy(0, 0, 0, sl).wait()
            return c
        lax.fori_loop(0, total, body, 0)

    @pl.when(i == 0)
    def _():
        buf[...] = jnp.zeros_like(buf)
        issue(0, 0)

    @pl.when(i < last)
    def _():
        issue(i + 1, 1 - slot)

    wait(i, slot)
    starts, nblks, offs, total = plan(i)
    pos = jnp.concatenate([pos_ref[...], jnp.zeros((LANES - N_EXPERTS, CMB_TT), F32)], axis=0).T
    cols = []
    for e in range(N_EXPERTS):
        pe = pos[:, e:e + 1]
        shift = (offs[e] * CMB_BLK - starts[e]).astype(F32)
        col = jnp.where(pe >= 0.0, pe + shift, -1.0)
        cols.append(jnp.broadcast_to(col, (CMB_TT, CMB_KC)))
    acc_ref[...] = x_ref[...]

    def chunk(k, c):
        k0 = pl.multiple_of(k * CMB_KC, CMB_KC)
        r = (lax.broadcasted_iota(jnp.int32, (CMB_TT, CMB_KC), 1) + k0).astype(F32)
        hit = jnp.where(cols[0] == r, 1.0, 0.0)
        for e in range(1, N_EXPERTS):
            hit = hit + jnp.where(cols[e] == r, 1.0, 0.0)
        acc_ref[...] += _dot(hit.astype(BF16), buf[slot, pl.ds(k0, CMB_KC), :])
        return c

    lax.fori_loop(0, (total * CMB_BLK + (CMB_KC - 1)) // CMB_KC, chunk, 0)
    o_ref[...] = _rms(acc_ref[...], gfin_ref[...]) if final else acc_ref[...]


def _combine(rowpre_flat, x2, pos, g_final, ye, g, final):
    n, d = x2.shape
    assert n == g * CMB_TT
    return pl.pallas_call(
        functools.partial(_combine_kernel, g=g, final=final),
        grid_spec=pltpu.PrefetchScalarGridSpec(
            num_scalar_prefetch=1,
            grid=(g,),
            in_specs=[pl.BlockSpec((CMB_TT, d), lambda i, rp: (i, 0)),
                      pl.BlockSpec((N_EXPERTS, CMB_TT), lambda i, rp: (0, i)),
                      pl.BlockSpec((1, d), lambda i, rp: (0, 0)),
                      pl.BlockSpec(memory_space=pl.ANY)],
            out_specs=pl.BlockSpec((CMB_TT, d), lambda i, rp: (i, 0)),
            scratch_shapes=[pltpu.VMEM((2, CMB_ROWS, d), BF16), pltpu.SemaphoreType.DMA((2,)),
                            pltpu.VMEM((CMB_TT, d), F32)]),
        out_shape=jax.ShapeDtypeStruct((n, d), F32),
        compiler_params=_cparams("arbitrary"),
    )(rowpre_flat, x2, pos, g_final, ye)


def _rot_half_cols(w):
    half = w.shape[-1] // 2
    return jnp.concatenate([-w[..., half:], w[..., :half]], axis=-1)


def _prep_layer(l, g_attn, w_in, sink, g_cq, w_uq, g_ckv, w_uk, w_uv, g_ya, g_yb, w_out, g_ffn,
                w_router):
    wi = w_in[l]
    w_kr = wi[:, SRC_KR:SRC_KR + QK_ROPE]
    zpad = jnp.zeros((D_MODEL, 128 - QK_ROPE), F32)
    w_in_ext = jnp.concatenate([wi[:, SRC_KA:SRC_VA], wi[:, SRC_CQ:SRC_KR], w_kr, zpad,
                                _rot_half_cols(w_kr), zpad], axis=1)
    w_in_t = jnp.concatenate([wi[:, SRC_QA:SRC_KA], wi[:, SRC_VA:SRC_CQ]], axis=1).T
    wq = w_uq[l].reshape(Q_LORA, MLA_HEADS, QK_NOPE + QK_ROPE)
    wq_rope = wq[:, :, QK_NOPE:]
    zq = jnp.zeros((Q_LORA, MLA_HEADS, 128 - QK_ROPE), F32)
    w_uq_main = jnp.concatenate([wq, zq], axis=-1).reshape(Q_LORA, MLA_HEADS * MLA_QK)
    w_uq_rot = jnp.concatenate([_rot_half_cols(wq_rope), zq], axis=-1).reshape(Q_LORA, MLA_HEADS * 128)
    return dict(
        g_attn=g_attn[l][None], w_in=w_in_ext.astype(BF16), w_in_t=w_in_t.astype(BF16), sink=sink[l],
        g_cq=g_cq[l][None], w_uq_main_t=w_uq_main.T.astype(BF16), w_uq_rot_t=w_uq_rot.T.astype(BF16),
        g_ckv=g_ckv[l][None], w_uk=w_uk[l].astype(BF16), w_uvt=w_uv[l].T.astype(BF16),
        g_ya=g_ya[l][None], g_yb=g_yb[l][None],
        w_out_a=w_out[l][:SWA_WIDTH].astype(BF16), w_out_b=w_out[l][SWA_WIDTH:].astype(BF16),
        g_ffn=g_ffn[l][None], w_router_t=w_router[l].T.astype(BF16), layer=l,
    )


def _rope_tables(s):
    inv = 1.0 / (ROPE_THETA ** (jnp.arange(0, QK_ROPE, 2, dtype=F32) / QK_ROPE))
    ang = jnp.arange(s, dtype=F32)[:, None] * inv[None, :]
    z = jnp.zeros((s, 128 - QK_ROPE), F32)
    cos, sin = jnp.cos(ang), jnp.sin(ang)
    cos128 = jnp.concatenate([cos, cos, z], axis=1)
    sin128 = jnp.concatenate([sin, sin, z], axis=1)
    return cos128, sin128, cos128.T, sin128.T


def _pick(n, pref):
    t = pref
    while n % t:
        t //= 2
    return t


def _layer(x, lw, experts, tables, g_final, final):
    b, s, d = x.shape
    n = b * s
    qat, ka, vat, qmt, km, vt = _in_proj(x, lw, tables, _pick(s, MLA_TK))
    ya = _swa(qat, ka, vat, lw["sink"])
    yb = _mla(qmt, km, vt)
    x2, h2, afft = _out_proj(x.reshape(n, d), ya.reshape(n, SWA_WIDTH), yb.reshape(n, MLA_WIDTH), lw,
                             _pick(n, 512))
    cap = CAPACITY_FACTOR * n // N_EXPERTS
    g = n // LANES
    idx, gate, pos, rowpre = _select(afft, cap)
    ye = _ffn(idx.reshape(-1), h2, *experts, lw["layer"], gate.reshape(N_EXPERTS, cap, 1), cap)
    bounds = jnp.concatenate([rowpre.reshape(N_EXPERTS, g), jnp.full((N_EXPERTS, 1), cap, F32)], axis=1)
    x2 = _combine(bounds.astype(jnp.int32).reshape(-1), x2, pos.reshape(N_EXPERTS, n), g_final, ye, g, final)
    return x2.reshape(b, s, d)


def _trunk(x, layers, experts, g_final):
    b, s, d = x.shape
    tables = _rope_tables(s)
    for lw in layers:
        x = _layer(x, lw, experts, tables, g_final[None], final=lw is layers[-1])
    return x


def kernel(x_prompt, x_sample, g_attn, w_in, sink, g_cq, w_uq, g_ckv, w_uk, w_uv, g_ya, g_yb, w_out,
           g_ffn, w_router, w_gate, w_up, w_down, g_final):
    depth = w_in.shape[0]
    layers = [_prep_layer(l, g_attn, w_in, sink, g_cq, w_uq, g_ckv, w_uk, w_uv, g_ya, g_yb, w_out,
                          g_ffn, w_router) for l in range(depth)]
    experts = _cast_experts(w_gate, w_up, w_down)
    return _trunk(x_prompt, layers, experts, g_final), _trunk(x_sample, layers, experts, g_final)
```

```python
import functools

import numpy as np
import jax
import jax.numpy as jnp
from jax import lax
from jax.experimental import pallas as pl
from jax.experimental.pallas import tpu as pltpu

F32 = jnp.float32
BF16 = jnp.bfloat16

D_MODEL = 2048
HEAD_DIM = 128
SWA_HEADS = 8
SWA_KV_HEADS = 2
SWA_GROUP = SWA_HEADS // SWA_KV_HEADS
WINDOW = 128
BLOCK = 128
MLA_HEADS = 8
QK_NOPE = 128
QK_ROPE = 64
V_DIM = 128
Q_LORA = 512
KV_LORA = 512
ROPE_THETA = 10000.0
SWA_WIDTH = SWA_HEADS * HEAD_DIM
SWA_KV_WIDTH = SWA_KV_HEADS * HEAD_DIM
MLA_WIDTH = MLA_HEADS * V_DIM
N_EXPERTS = 16
D_EXPERT = 1408
CAPACITY_FACTOR = 2
EPS = 1e-6
LANES = 128

SRC_QA = 0
SRC_KA = SRC_QA + SWA_WIDTH
SRC_VA = SRC_KA + SWA_KV_WIDTH
SRC_CQ = SRC_VA + SWA_KV_WIDTH
SRC_KR = SRC_CQ + Q_LORA + KV_LORA
OFF_KA = 0
OFF_CQ = OFF_KA + SWA_KV_WIDTH
OFF_CKV = OFF_CQ + Q_LORA
OFF_KR = OFF_CKV + KV_LORA
OFF_KRR = OFF_KR + 128
IN_EXT = OFF_KRR + 128
TOFF_QA = 0
TOFF_VA = TOFF_QA + SWA_WIDTH
IN_T = TOFF_VA + SWA_KV_WIDTH
MLA_QK = 256

SWA_SCALE = HEAD_DIM ** -0.5
MLA_SCALE = (QK_NOPE + QK_ROPE) ** -0.5 * float(np.log2(np.e))
NEG_BIG = -1e30
MLA_TQ = 512
MLA_TK = 512

V7X_VMEM_BYTES = 64 * 1024 * 1024
VMEM_LIMIT = V7X_VMEM_BYTES - 8 * 1024 * 1024


def _cparams(*sem):
    return pltpu.CompilerParams(dimension_semantics=sem, vmem_limit_bytes=VMEM_LIMIT)


def _const_spec(shape):
    nd = len(shape)
    return pl.BlockSpec(shape, lambda *_: (0,) * nd, pipeline_mode=pl.Buffered(1))


def _rms(x, g):
    return x * lax.rsqrt(jnp.mean(x * x, axis=-1, keepdims=True) + EPS) * g


def _dot(a, b):
    return jnp.dot(a, b, preferred_element_type=F32)


def _dot_nt(a, b):
    return lax.dot_general(a, b, (((1,), (1,)), ((), ())), preferred_element_type=F32)


def _in_proj_kernel(x_ref, g_ref, win_ref, wint_ref, gcq_ref, wqmt_ref, wqrt_ref, gckv_ref, wuk_ref,
                    wuvt_ref, cos_ref, sin_ref, cost_ref, sint_ref,
                    qat_ref, ka_ref, vat_ref, qmt_ref, km_ref, vt_ref):
    h = _rms(x_ref[...], g_ref[...]).astype(BF16)
    proj = _dot(h, win_ref[...])
    proj_t = _dot_nt(wint_ref[...], h)
    qat_ref[...] = (proj_t[TOFF_QA:TOFF_VA] * SWA_SCALE).astype(BF16)
    vat_ref[...] = proj_t[TOFF_VA:IN_T].astype(BF16)
    ka_ref[...] = proj[:, OFF_KA:OFF_CQ].astype(BF16)
    cos = cos_ref[...]
    sin = sin_ref[...]
    krope = (proj[:, OFF_KR:OFF_KRR] * cos + proj[:, OFF_KRR:IN_EXT] * sin).astype(BF16)
    cqn = _rms(proj[:, OFF_CQ:OFF_CKV], gcq_ref[...]).astype(BF16)
    qmain = _dot_nt(wqmt_ref[...], cqn)
    qrot = _dot_nt(wqrt_ref[...], cqn)
    cos_t = cost_ref[...]
    sin_t = sint_ref[...]
    for hh in range(MLA_HEADS):
        lo = hh * MLA_QK
        qmt_ref[lo:lo + QK_NOPE, :] = (qmain[lo:lo + QK_NOPE] * MLA_SCALE).astype(BF16)
        rp = qmain[lo + QK_NOPE:lo + MLA_QK] * cos_t + qrot[hh * 128:(hh + 1) * 128] * sin_t
        qmt_ref[lo + QK_NOPE:lo + MLA_QK, :] = (rp * MLA_SCALE).astype(BF16)
    ckvn = _rms(proj[:, OFF_CKV:OFF_KR], gckv_ref[...]).astype(BF16)
    kn = _dot(ckvn, wuk_ref[...])
    for hh in range(MLA_HEADS):
        lo = hh * MLA_QK
        km_ref[:, lo:lo + QK_NOPE] = kn[:, hh * QK_NOPE:(hh + 1) * QK_NOPE].astype(BF16)
        km_ref[:, lo + QK_NOPE:lo + MLA_QK] = krope
    vt_ref[...] = _dot_nt(wuvt_ref[...], ckvn).astype(BF16)


def _in_proj(x, lw, tables, tm):
    b, s, d = x.shape
    row = lambda w: pl.BlockSpec((None, tm, w), lambda i, j: (i, j, 0))
    tab = pl.BlockSpec((tm, 128), lambda i, j: (j, 0))
    tab_t = pl.BlockSpec((128, tm), lambda i, j: (0, j))
    col = lambda w: pl.BlockSpec((None, w, tm), lambda i, j: (i, 0, j))
    out_shapes = (
        jax.ShapeDtypeStruct((b, SWA_WIDTH, s), BF16),
        jax.ShapeDtypeStruct((b, s, SWA_KV_WIDTH), BF16),
        jax.ShapeDtypeStruct((b, SWA_KV_WIDTH, s), BF16),
        jax.ShapeDtypeStruct((b, MLA_HEADS * MLA_QK, s), BF16),
        jax.ShapeDtypeStruct((b, s, MLA_HEADS * MLA_QK), BF16),
        jax.ShapeDtypeStruct((b, s // tm, MLA_WIDTH, tm), BF16),
    )
    return pl.pallas_call(
        _in_proj_kernel,
        grid=(b, s // tm),
        in_specs=[
            row(d),
            _const_spec((1, d)),
            _const_spec((d, IN_EXT)),
            _const_spec((IN_T, d)),
            _const_spec((1, Q_LORA)),
            _const_spec((MLA_HEADS * MLA_QK, Q_LORA)),
            _const_spec((MLA_HEADS * 128, Q_LORA)),
            _const_spec((1, KV_LORA)),
            _const_spec((KV_LORA, MLA_HEADS * QK_NOPE)),
            _const_spec((MLA_WIDTH, KV_LORA)),
            tab, tab, tab_t, tab_t,
        ],
        out_specs=(col(SWA_WIDTH), row(SWA_KV_WIDTH), col(SWA_KV_WIDTH), col(MLA_HEADS * MLA_QK),
                   row(MLA_HEADS * MLA_QK),
                   pl.BlockSpec((None, None, MLA_WIDTH, tm), lambda i, j: (i, j, 0, 0))),
        out_shape=out_shapes,
        compiler_params=_cparams("parallel", "parallel"),
    )(x, lw["g_attn"], lw["w_in"], lw["w_in_t"], lw["g_cq"], lw["w_uq_main_t"], lw["w_uq_rot_t"], lw["g_ckv"],
      lw["w_uk"], lw["w_uvt"], *tables)


def _alibi_slopes():
    return [float(np.float32(2.0 ** (-8.0 * (i + 1) / SWA_HEADS))) for i in range(SWA_HEADS)]


SWA_TQ = 512


def _swa_kernel(sink_ref, qt_ref, kp_ref, kc_ref, kn_ref, vtp_ref, vtc_ref, vtn_ref, o_ref, *, seq):
    tq = qt_ref.shape[1]
    base = pl.program_id(1) * tq
    ki = lax.broadcasted_iota(jnp.int32, (3 * BLOCK, BLOCK), 0)
    qj = lax.broadcasted_iota(jnp.int32, (3 * BLOCK, BLOCK), 1)
    rel = ki - BLOCK - qj
    in_band = jnp.abs(rel) <= WINDOW
    dist = jnp.abs(rel).astype(F32)
    slopes = _alibi_slopes()
    kall = jnp.concatenate([kp_ref[...], kc_ref[...], kn_ref[...]], axis=0)
    vtall = jnp.concatenate([vtp_ref[...], vtc_ref[...], vtn_ref[...]], axis=1)
    for sb in range(tq // BLOCK):
        spos = base + sb * BLOCK + ki - BLOCK
        valid = in_band & (spos >= 0) & (spos < seq)
        cols = slice(sb * BLOCK, (sb + 1) * BLOCK)
        win = slice(sb * BLOCK, (sb + 3) * BLOCK)
        for kvh in range(SWA_KV_HEADS):
            kh = kall[win, kvh * HEAD_DIM:(kvh + 1) * HEAD_DIM]
            vth = vtall[kvh * HEAD_DIM:(kvh + 1) * HEAD_DIM, win]
            heads = [kvh * SWA_GROUP + g for g in range(SWA_GROUP)]
            qt = jnp.concatenate([qt_ref[hd * HEAD_DIM:(hd + 1) * HEAD_DIM, cols] for hd in heads], axis=1)
            raw = _dot(kh, qt)
            logits = jnp.concatenate(
                [jnp.where(valid, raw[:, g * BLOCK:(g + 1) * BLOCK] - slopes[hd] * dist, NEG_BIG)
                 for g, hd in enumerate(heads)], axis=1)
            sk = jnp.concatenate([jnp.full((1, BLOCK), sink_ref[hd], F32) for hd in heads], axis=1)
            m = jnp.maximum(jnp.max(logits, axis=0, keepdims=True), sk)
            p = jnp.exp(logits - m)
            denom = jnp.sum(p, axis=0, keepdims=True) + jnp.exp(sk - m)
            ot = _dot(vth, p.astype(BF16)) / denom
            for g, hd in enumerate(heads):
                o_ref[cols, hd * HEAD_DIM:(hd + 1) * HEAD_DIM] = ot[:, g * BLOCK:(g + 1) * BLOCK].T.astype(BF16)


def _swa(qat, ka, vat, sink):
    b, s, _ = ka.shape
    tq = _pick(s, SWA_TQ)
    r = tq // BLOCK
    nb = s // BLOCK
    prev = lambda j: jnp.maximum(j * r - 1, 0)
    nxt = lambda j: jnp.minimum(j * r + r, nb - 1)
    kedge = lambda f: pl.BlockSpec((None, BLOCK, SWA_KV_WIDTH), lambda i, j: (i, f(j), 0))
    vedge = lambda f: pl.BlockSpec((None, SWA_KV_WIDTH, BLOCK), lambda i, j: (i, 0, f(j)))
    return pl.pallas_call(
        functools.partial(_swa_kernel, seq=s),
        grid=(b, s // tq),
        in_specs=[pl.BlockSpec(memory_space=pltpu.SMEM),
                  pl.BlockSpec((None, SWA_WIDTH, tq), lambda i, j: (i, 0, j)),
                  kedge(prev), pl.BlockSpec((None, tq, SWA_KV_WIDTH), lambda i, j: (i, j, 0)), kedge(nxt),
                  vedge(prev), pl.BlockSpec((None, SWA_KV_WIDTH, tq), lambda i, j: (i, 0, j)), vedge(nxt)],
        out_specs=pl.BlockSpec((None, tq, SWA_WIDTH), lambda i, j: (i, j, 0)),
        out_shape=jax.ShapeDtypeStruct((b, s, SWA_WIDTH), BF16),
        compiler_params=_cparams("parallel", "parallel"),
    )(sink, qat, ka, ka, ka, vat, vat, vat)


MLA_NSUB = 2
MLA_SUM_ROWS = 16


def _mla_kernel(q_ref, k_ref, vt_ref, o_ref, st_ref):
    nsub, _, tk, tq = st_ref.shape
    nchunk = vt_ref.shape[0]
    assert nchunk % 2 == 0
    qs = [q_ref[:, t * tq:(t + 1) * tq] for t in range(nsub)]

    def produce(t, slot, c):
        k0 = pl.multiple_of(c * tk, tk)
        st = _dot(k_ref[pl.ds(k0, tk), :], qs[t])
        st_ref[t, slot] = st
        return jnp.max(st, axis=0, keepdims=True)

    ones_rows = jnp.ones((MLA_SUM_ROWS, tk), BF16)

    def consume(t, slot, c, mx, state):
        m, acc = state
        m_new = jnp.maximum(m, mx)
        alpha = jnp.exp2(m - m_new)
        p = jnp.exp2(st_ref[t, slot] - m_new)
        vext = jnp.concatenate([vt_ref[c], ones_rows], axis=0)
        acc = alpha * acc + _dot(vext, p.astype(BF16))
        return m_new, acc

    def pair(c, mx0, states, produce_next):
        mx1 = [produce(t, 1, c + 1) for t in range(nsub)]
        states = [consume(t, 0, c, mx0[t], states[t]) for t in range(nsub)]
        if produce_next:
            mx0 = [produce(t, 0, c + 2) for t in range(nsub)]
        states = [consume(t, 1, c + 1, mx1[t], states[t]) for t in range(nsub)]
        return mx0, states

    def body(i, carry):
        return pair(2 * i, *carry, produce_next=True)

    state0 = (jnp.full((1, tq), NEG_BIG, F32), jnp.zeros((V_DIM + MLA_SUM_ROWS, tq), F32))
    init = ([produce(t, 0, 0) for t in range(nsub)], [state0] * nsub)
    mx0, states = lax.fori_loop(0, nchunk // 2 - 1, body, init, unroll=2)
    _, states = pair(nchunk - 2, mx0, states, produce_next=False)
    for t in range(nsub):
        acc = states[t][1]
        o_ref[t * tq:(t + 1) * tq, :] = (acc[:V_DIM] / acc[V_DIM:V_DIM + 1]).T.astype(BF16)


def _mla(qmt, km, vt):
    b, s, _ = km.shape
    nchunk, tk = vt.shape[1], vt.shape[3]
    tq = _pick(s, MLA_TQ)
    nsub = MLA_NSUB if s % (MLA_NSUB * tq) == 0 else 1
    return pl.pallas_call(
        _mla_kernel,
        grid=(b, MLA_HEADS, s // (nsub * tq)),
        in_specs=[pl.BlockSpec((None, MLA_QK, nsub * tq), lambda bb, h, i: (bb, h, i)),
                  pl.BlockSpec((None, s, MLA_QK), lambda bb, h, i: (bb, 0, h)),
                  pl.BlockSpec((None, nchunk, V_DIM, tk), lambda bb, h, i: (bb, 0, h, 0))],
        out_specs=pl.BlockSpec((None, nsub * tq, V_DIM), lambda bb, h, i: (bb, i, h)),
        out_shape=jax.ShapeDtypeStruct((b, s, MLA_WIDTH), BF16),
        scratch_shapes=[pltpu.VMEM((nsub, 2, tk, tq), F32)],
        compiler_params=_cparams("parallel", "parallel", "arbitrary"),
    )(qmt, km, vt)


H2_ROWS = D_MODEL // 2 // LANES


def _out_proj_kernel(x_ref, ya_ref, yb_ref, gya_ref, gyb_ref, woa_ref, wob_ref, gffn_ref, wrt_ref,
                     xo_ref, h2_ref, afft_ref):
    yan = _rms(ya_ref[...].astype(F32), gya_ref[...]).astype(BF16)
    ybn = _rms(yb_ref[...].astype(F32), gyb_ref[...]).astype(BF16)
    xn = x_ref[...] + _dot(yan, woa_ref[...]) + _dot(ybn, wob_ref[...])
    xo_ref[...] = xn
    h2 = _rms(xn, gffn_ref[...]).astype(BF16)
    half = h2.shape[1] // 2
    hi = pltpu.bitcast(h2[:, :half].astype(F32), jnp.uint32)
    lo = pltpu.bitcast(h2[:, half:].astype(F32), jnp.uint32)
    packed = hi | (lo >> 16)
    tm = packed.shape[0]
    for c in range(H2_ROWS):
        h2_ref[pl.ds(c, tm, stride=H2_ROWS), :] = packed[:, c * LANES:(c + 1) * LANES]
    logits = _dot_nt(wrt_ref[...], h2)
    m = jnp.max(logits, axis=0, keepdims=True)
    p = jnp.exp(logits - m)
    afft_ref[...] = p / jnp.sum(p, axis=0, keepdims=True)


def _out_proj(x2, ya2, yb2, lw, tm):
    n, d = x2.shape
    row = lambda w: pl.BlockSpec((tm, w), lambda i: (i, 0))
    return pl.pallas_call(
        _out_proj_kernel,
        grid=(n // tm,),
        in_specs=[row(d), row(SWA_WIDTH), row(MLA_WIDTH),
                  _const_spec((1, SWA_WIDTH)), _const_spec((1, MLA_WIDTH)),
                  _const_spec((SWA_WIDTH, d)), _const_spec((MLA_WIDTH, d)),
                  _const_spec((1, d)), _const_spec((N_EXPERTS, d))],
        out_specs=(row(d), pl.BlockSpec((tm * H2_ROWS, LANES), lambda i: (i, 0)),
                   pl.BlockSpec((N_EXPERTS, tm), lambda i: (0, i))),
        out_shape=(jax.ShapeDtypeStruct((n, d), F32), jax.ShapeDtypeStruct((n * H2_ROWS, LANES), jnp.uint32),
                   jax.ShapeDtypeStruct((N_EXPERTS, n), F32)),
        compiler_params=_cparams("parallel"),
    )(x2, ya2, yb2, lw["g_ya"], lw["g_yb"], lw["w_out_a"], lw["w_out_b"], lw["g_ffn"], lw["w_router_t"])


SEL_CJ = 1024


def _token_cumsum(mask, upper, ones, lstrict):
    mb = mask.astype(BF16)
    within = _dot(mb, upper)
    tot = _dot(mb, ones)
    rowpre = _dot(lstrict, tot.astype(BF16))
    return within, tot, rowpre


def _select_kernel(a_ref, idx_ref, gate_ref, pos_ref, rowpre_ref, *, cap):
    a = a_ref[...]
    g = a.shape[0]
    bits = pltpu.bitcast(a, jnp.int32)

    def count_ge(cand):
        return jnp.sum(jnp.where(bits >= cand, 1.0, 0.0), keepdims=True)

    def radix_step(i, t):
        hi = jnp.left_shift(jnp.int32(1), 29 - 2 * i)
        lo = jnp.left_shift(jnp.int32(1), 28 - 2 * i)
        c3, c2, c1 = t | hi | lo, t | hi, t | lo
        return jnp.where(count_ge(c3) >= cap, c3,
                         jnp.where(count_ge(c2) >= cap, c2, jnp.where(count_ge(c1) >= cap, c1, t)))

    top = jnp.full((1, 1), 1 << 30, jnp.int32)
    thr = jnp.where(count_ge(top) >= cap, top, jnp.zeros((1, 1), jnp.int32))
    thr = lax.fori_loop(0, 15, radix_step, thr)
    ii = lax.broadcasted_iota(jnp.int32, (LANES, LANES), 0)
    jj = lax.broadcasted_iota(jnp.int32, (LANES, LANES), 1)
    upper = jnp.where(ii <= jj, 1.0, 0.0).astype(BF16)
    ones = jnp.ones((LANES, LANES), BF16)
    gi = lax.broadcasted_iota(jnp.int32, (g, g), 0)
    gj = lax.broadcasted_iota(jnp.int32, (g, g), 1)
    lstrict = jnp.where(gj < gi, 1.0, 0.0).astype(BF16)

    gt = bits > thr
    eq = bits == thr
    need = cap - jnp.sum(jnp.where(gt, 1.0, 0.0), keepdims=True)
    eqf = jnp.where(eq, 1.0, 0.0)
    w_eq, _, rp_eq = _token_cumsum(eqf, upper, ones, lstrict)
    tie_rank = w_eq - eqf + rp_eq
    self_ = jnp.where(gt, 1.0, jnp.where(tie_rank < need, eqf, 0.0))
    w, tot, rp = _token_cumsum(self_, upper, ones, lstrict)
    pos_ref[...] = jnp.where(self_ > 0.0, w - 1.0 + rp, -1.0)
    rpt = rp.T
    rowpre_ref[...] = rpt[0:1, :]

    rc = rp + tot
    wt = w.T.astype(BF16)
    pre_hi = jnp.floor(rpt[0:16] * (1.0 / 64.0))
    pre_lo = rpt[0:16] - 64.0 * pre_hi
    pre_hi = pre_hi.astype(BF16)
    pre_lo = pre_lo.astype(BF16)
    at = a.T
    a_h = at.astype(BF16)
    r1 = at - a_h.astype(F32)
    a_m = r1.astype(BF16)
    a_l = (r1 - a_m.astype(F32)).astype(BF16)
    cj_n = min(SEL_CJ, cap)
    for jc in range(cap // cj_n):
        j = (lax.broadcasted_iota(jnp.int32, (1, cj_n), 1) + jc * cj_n).astype(F32)
        rct = jnp.tile(rc, (1, cj_n // LANES))
        row = jnp.sum(jnp.where(rct <= j, 1.0, 0.0), axis=0, keepdims=True)
        grow = lax.broadcasted_iota(jnp.int32, (g, cj_n), 0).astype(F32)
        onehot = jnp.where(grow == row, 1.0, 0.0).astype(BF16)
        wg = _dot(wt, onehot)
        pre = 64.0 * _dot(pre_hi, onehot)[0:1] + _dot(pre_lo, onehot)[0:1]
        jloc = j - pre
        lane = jnp.sum(jnp.where(wg <= jloc, 1.0, 0.0), axis=0, keepdims=True)
        idx_ref[:, jc * cj_n:(jc + 1) * cj_n] = (row * LANES + lane).astype(jnp.int32)
        ag = _dot(a_h, onehot) + _dot(a_m, onehot) + _dot(a_l, onehot)
        lrow = lax.broadcasted_iota(jnp.int32, (LANES, cj_n), 0).astype(F32)
        gate_ref[:, jc * cj_n:(jc + 1) * cj_n] = jnp.sum(jnp.where(lrow == lane, ag, 0.0), axis=0,
                                                          keepdims=True)


def _select(afft, cap):
    e, n = afft.shape
    g = n // LANES
    blk = lambda w: pl.BlockSpec((None, 1, w), lambda i: (i, 0, 0))
    tok = pl.BlockSpec((None, g, LANES), lambda i: (i, 0, 0))
    return pl.pallas_call(
        functools.partial(_select_kernel, cap=cap),
        grid=(e,),
        in_specs=[tok],
        out_specs=(blk(cap), blk(cap), tok, blk(g)),
        out_shape=(jax.ShapeDtypeStruct((e, 1, cap), jnp.int32), jax.ShapeDtypeStruct((e, 1, cap), F32),
                   jax.ShapeDtypeStruct((e, g, LANES), F32), jax.ShapeDtypeStruct((e, 1, g), F32)),
        compiler_params=_cparams("parallel"),
    )(afft.reshape(e, g, LANES))


FFN_TT = 256
FFN_SPLIT = 2
FFN_SLOTS = 3


def _ffn_kernel(idx_ref, h2_hbm, wgu_ref, wd_ref, gate_ref, ye_ref, xbuf, sem):
    tt = xbuf.shape[1] // H2_ROWS
    nj = pl.num_programs(1)
    step = pl.program_id(0) * nj + pl.program_id(1)
    last = pl.num_programs(0) * nj - 1
    nslot = xbuf.shape[0]
    ahead = nslot - 1
    slot = step % nslot

    def row_copy(tok, sl, r):
        return pltpu.make_async_copy(h2_hbm.at[pl.ds(pl.multiple_of(tok * H2_ROWS, H2_ROWS), H2_ROWS), :],
                                     xbuf.at[sl, pl.ds(r * H2_ROWS, H2_ROWS), :], sem.at[sl])

    def issue(s):
        sl = s % nslot
        for r in range(tt):
            row_copy(idx_ref[s * tt + r], sl, r).start()

    def wait_slot(sl):
        pltpu.make_async_copy(h2_hbm.at[pl.ds(0, tt * H2_ROWS), :], xbuf.at[sl], sem.at[sl]).wait()

    @pl.when(step == 0)
    def _():
        for s in range(ahead):
            @pl.when(s <= last)
            def _():
                issue(s)

    @pl.when(step + ahead <= last)
    def _():
        issue(step + ahead)

    wait_slot(slot)
    words = [xbuf[slot, pl.ds(c, tt, stride=H2_ROWS), :] for c in range(H2_ROWS)]
    x = jnp.concatenate([pltpu.bitcast(w & jnp.uint32(0xFFFF0000), F32).astype(BF16) for w in words]
                        + [pltpu.bitcast(w << 16, F32).astype(BF16) for w in words], axis=1)
    for part in range(FFN_SPLIT):
        r = slice(part * (tt // FFN_SPLIT), (part + 1) * (tt // FFN_SPLIT))
        au = _dot(x[r], wgu_ref[...])
        a = au[:, :D_EXPERT]
        u = au[:, D_EXPERT:]
        act = (a / (1.0 + jnp.exp(-a)) * u).astype(BF16)
        ye_ref[r, :] = (_dot(act, wd_ref[...]) * gate_ref[r, :]).astype(BF16)


def _ffn(idx_flat, h2p, wgu, wd, layer, gate, cap):
    e, d = wgu.shape[1], wgu.shape[2]
    tt = _pick(cap, FFN_TT)
    return pl.pallas_call(
        _ffn_kernel,
        grid_spec=pltpu.PrefetchScalarGridSpec(
            num_scalar_prefetch=1,
            grid=(e, cap // tt),
            in_specs=[pl.BlockSpec(memory_space=pl.ANY),
                      pl.BlockSpec((None, None, d, 2 * D_EXPERT), lambda i, j, idx: (layer, i, 0, 0)),
                      pl.BlockSpec((None, None, D_EXPERT, d), lambda i, j, idx: (layer, i, 0, 0)),
                      pl.BlockSpec((None, tt, 1), lambda i, j, idx: (i, j, 0))],
            out_specs=pl.BlockSpec((None, tt, d), lambda i, j, idx: (i, j, 0)),
            scratch_shapes=[pltpu.VMEM((FFN_SLOTS, tt * H2_ROWS, LANES), jnp.uint32),
                            pltpu.SemaphoreType.DMA((FFN_SLOTS,))]),
        out_shape=jax.ShapeDtypeStruct((e, cap, d), BF16),
        compiler_params=_cparams("arbitrary", "arbitrary"),
    )(idx_flat, h2p, wgu, wd, gate)


CAST_ROWS = 1024
BF16_SUBLANES = 16


def _row_block(n, limit):
    return max(r for r in range(BF16_SUBLANES, limit + 1, BF16_SUBLANES) if n % r == 0)


def _cast_gu_kernel(g_ref, u_ref, o_ref):
    o_ref[:, :D_EXPERT] = g_ref[...].astype(BF16)
    o_ref[:, D_EXPERT:] = u_ref[...].astype(BF16)


def _cast_kernel(w_ref, o_ref):
    o_ref[...] = w_ref[...].astype(BF16)


def _cast_experts(w_gate, w_up, w_down):
    l, e, d, f = w_gate.shape
    blk = lambda r, c: pl.BlockSpec((None, None, r, c), lambda a, b, i: (a, b, i, 0))
    rg = _row_block(d, CAST_ROWS // 2)
    wgu = pl.pallas_call(
        _cast_gu_kernel, grid=(l, e, d // rg),
        in_specs=[blk(rg, f), blk(rg, f)], out_specs=blk(rg, 2 * f),
        out_shape=jax.ShapeDtypeStruct((l, e, d, 2 * f), BF16),
        compiler_params=_cparams("parallel", "parallel", "parallel"),
    )(w_gate, w_up)
    rd = _row_block(f, CAST_ROWS)
    wd = pl.pallas_call(
        _cast_kernel, grid=(l, e, f // rd),
        in_specs=[blk(rd, d)], out_specs=blk(rd, d),
        out_shape=jax.ShapeDtypeStruct((l, e, f, d), BF16),
        compiler_params=_cparams("parallel", "parallel", "parallel"),
    )(w_down)
    return wgu, wd


CMB_TT = 128
CMB_BLK = 16
CMB_KC = 256
CMB_ROWS = N_EXPERTS * CMB_TT + N_EXPERTS * 2 * CMB_BLK + CMB_KC


def _combine_kernel(rowpre_ref, x_ref, pos_ref, gfin_ref, ye_hbm, o_ref, buf, sem, acc_ref, *, g, final):
    i = pl.program_id(0)
    last = pl.num_programs(0) - 1
    slot = i % 2

    def plan(tile):
        starts, nblks, offs = [], [], []
        off = jnp.int32(0)
        for e in range(N_EXPERTS):
            lo = rowpre_ref[e * (g + 1) + tile]
            hi = rowpre_ref[e * (g + 1) + tile + 1]
            start = (lo >> 4) << 4
            nblk = jnp.where(hi > lo, (hi - start + (CMB_BLK - 1)) >> 4, 0)
            starts.append(start)
            nblks.append(nblk)
            offs.append(off)
            off = off + nblk
        return starts, nblks, offs, off

    def block_copy(e, src_row, dst_row, sl):
        return pltpu.make_async_copy(ye_hbm.at[e, pl.ds(src_row, CMB_BLK), :],
                                     buf.at[sl, pl.ds(dst_row, CMB_BLK), :], sem.at[sl])

    def issue(tile, sl):
        starts, nblks, offs, _ = plan(tile)
        for e in range(N_EXPERTS):
            def body(b, c, e=e):
                block_copy(e, pl.multiple_of(starts[e] + b * CMB_BLK, CMB_BLK),
                           pl.multiple_of((offs[e] + b) * CMB_BLK, CMB_BLK), sl).start()
                return c
            lax.fori_loop(0, nblks[e], body, 0)

    def wait(tile, sl):
        rows = plan(tile)[3] * CMB_BLK

        @pl.when(rows > 0)
        def _():
            pltpu.make_async_copy(ye_hbm.at[0, pl.ds(0, rows), :], buf.at[sl, pl.ds(0, rows), :],
                                  sem.at[sl]).wait()

    @pl.when(i == 0)
    def _():
        buf[...] = jnp.zeros_like(buf)
        issue(0, 0)

    @pl.when(i < last)
    def _():
        issue(i + 1, 1 - slot)

    wait(i, slot)
    starts, nblks, offs, total = plan(i)
    pos = jnp.concatenate([pos_ref[...], jnp.zeros((LANES - N_EXPERTS, CMB_TT), F32)], axis=0).T
    cols = []
    for e in range(N_EXPERTS):
        pe = pos[:, e:e + 1]
        shift = (offs[e] * CMB_BLK - starts[e]).astype(F32)
        col = jnp.where(pe >= 0.0, pe + shift, -1.0)
        cols.append(jnp.broadcast_to(col, (CMB_TT, CMB_KC)))
    acc_ref[...] = x_ref[...]

    def chunk(k, c):
        k0 = pl.multiple_of(k * CMB_KC, CMB_KC)
        r = (lax.broadcasted_iota(jnp.int32, (CMB_TT, CMB_KC), 1) + k0).astype(F32)
        hit = jnp.where(cols[0] == r, 1.0, 0.0)
        for e in range(1, N_EXPERTS):
            hit = hit + jnp.where(cols[e] == r, 1.0, 0.0)
        acc_ref[...] += _dot(hit.astype(BF16), buf[slot, pl.ds(k0, CMB_KC), :])
        return c

    lax.fori_loop(0, (total * CMB_BLK + (CMB_KC - 1)) // CMB_KC, chunk, 0)
    o_ref[...] = _rms(acc_ref[...], gfin_ref[...]) if final else acc_ref[...]


def _combine(rowpre_flat, x2, pos, g_final, ye, g, final):
    n, d = x2.shape
    assert n == g * CMB_TT
    return pl.pallas_call(
        functools.partial(_combine_kernel, g=g, final=final),
        grid_spec=pltpu.PrefetchScalarGridSpec(
            num_scalar_prefetch=1,
            grid=(g,),
            in_specs=[pl.BlockSpec((CMB_TT, d), lambda i, rp: (i, 0)),
                      pl.BlockSpec((N_EXPERTS, CMB_TT), lambda i, rp: (0, i)),
                      pl.BlockSpec((1, d), lambda i, rp: (0, 0)),
                      pl.BlockSpec(memory_space=pl.ANY)],
            out_specs=pl.BlockSpec((CMB_TT, d), lambda i, rp: (i, 0)),
            scratch_shapes=[pltpu.VMEM((2, CMB_ROWS, d), BF16), pltpu.SemaphoreType.DMA((2,)),
                            pltpu.VMEM((CMB_TT, d), F32)]),
        out_shape=jax.ShapeDtypeStruct((n, d), F32),
        compiler_params=_cparams("arbitrary"),
    )(rowpre_flat, x2, pos, g_final, ye)


def _rot_half_cols(w):
    half = w.shape[-1] // 2
    return jnp.concatenate([-w[..., half:], w[..., :half]], axis=-1)


def _prep_layer(l, g_attn, w_in, sink, g_cq, w_uq, g_ckv, w_uk, w_uv, g_ya, g_yb, w_out, g_ffn,
                w_router):
    wi = w_in[l]
    w_kr = wi[:, SRC_KR:SRC_KR + QK_ROPE]
    zpad = jnp.zeros((D_MODEL, 128 - QK_ROPE), F32)
    w_in_ext = jnp.concatenate([wi[:, SRC_KA:SRC_VA], wi[:, SRC_CQ:SRC_KR], w_kr, zpad,
                                _rot_half_cols(w_kr), zpad], axis=1)
    w_in_t = jnp.concatenate([wi[:, SRC_QA:SRC_KA], wi[:, SRC_VA:SRC_CQ]], axis=1).T
    wq = w_uq[l].reshape(Q_LORA, MLA_HEADS, QK_NOPE + QK_ROPE)
    wq_rope = wq[:, :, QK_NOPE:]
    zq = jnp.zeros((Q_LORA, MLA_HEADS, 128 - QK_ROPE), F32)
    w_uq_main = jnp.concatenate([wq, zq], axis=-1).reshape(Q_LORA, MLA_HEADS * MLA_QK)
    w_uq_rot = jnp.concatenate([_rot_half_cols(wq_rope), zq], axis=-1).reshape(Q_LORA, MLA_HEADS * 128)
    return dict(
        g_attn=g_attn[l][None], w_in=w_in_ext.astype(BF16), w_in_t=w_in_t.astype(BF16), sink=sink[l],
        g_cq=g_cq[l][None], w_uq_main_t=w_uq_main.T.astype(BF16), w_uq_rot_t=w_uq_rot.T.astype(BF16),
        g_ckv=g_ckv[l][None], w_uk=w_uk[l].astype(BF16), w_uvt=w_uv[l].T.astype(BF16),
        g_ya=g_ya[l][None], g_yb=g_yb[l][None],
        w_out_a=w_out[l][:SWA_WIDTH].astype(BF16), w_out_b=w_out[l][SWA_WIDTH:].astype(BF16),
        g_ffn=g_ffn[l][None], w_router_t=w_router[l].T.astype(BF16), layer=l,
    )


def _rope_tables(s):
    inv = 1.0 / (ROPE_THETA ** (jnp.arange(0, QK_ROPE, 2, dtype=F32) / QK_ROPE))
    ang = jnp.arange(s, dtype=F32)[:, None] * inv[None, :]
    z = jnp.zeros((s, 128 - QK_ROPE), F32)
    cos, sin = jnp.cos(ang), jnp.sin(ang)
    cos128 = jnp.concatenate([cos, cos, z], axis=1)
    sin128 = jnp.concatenate([sin, sin, z], axis=1)
    return cos128, sin128, cos128.T, sin128.T


def _pick(n, pref):
    t = pref
    while n % t:
        t //= 2
    return t


def _layer(x, lw, experts, tables, g_final, final):
    b, s, d = x.shape
    n = b * s
    qat, ka, vat, qmt, km, vt = _in_proj(x, lw, tables, _pick(s, MLA_TK))
    ya = _swa(qat, ka, vat, lw["sink"])
    yb = _mla(qmt, km, vt)
    x2, h2, afft = _out_proj(x.reshape(n, d), ya.reshape(n, SWA_WIDTH), yb.reshape(n, MLA_WIDTH), lw,
                             _pick(n, 512))
    cap = CAPACITY_FACTOR * n // N_EXPERTS
    g = n // LANES
    idx, gate, pos, rowpre = _select(afft, cap)
    ye = _ffn(idx.reshape(-1), h2, *experts, lw["layer"], gate.reshape(N_EXPERTS, cap, 1), cap)
    bounds = jnp.concatenate([rowpre.reshape(N_EXPERTS, g), jnp.full((N_EXPERTS, 1), cap, F32)], axis=1)
    x2 = _combine(bounds.astype(jnp.int32).reshape(-1), x2, pos.reshape(N_EXPERTS, n), g_final, ye, g, final)
    return x2.reshape(b, s, d)


def _trunk(x, layers, experts, g_final):
    b, s, d = x.shape
    tables = _rope_tables(s)
    for lw in layers:
        x = _layer(x, lw, experts, tables, g_final[None], final=lw is layers[-1])
    return x


def kernel(x_prompt, x_sample, g_attn, w_in, sink, g_cq, w_uq, g_ckv, w_uk, w_uv, g_ya, g_yb, w_out,
           g_ffn, w_router, w_gate, w_up, w_down, g_final):
    depth = w_in.shape[0]
    layers = [_prep_layer(l, g_attn, w_in, sink, g_cq, w_uq, g_ckv, w_uk, w_uv, g_ya, g_yb, w_out,
                          g_ffn, w_router) for l in range(depth)]
    experts = _cast_experts(w_gate, w_up, w_down)
    return _trunk(x_prompt, layers, experts, g_final), _trunk(x_sample, layers, experts, g_final)
```

```python
import functools

import numpy as np
import jax
import jax.numpy as jnp
from jax import lax
from jax.experimental import pallas as pl
from jax.experimental.pallas import tpu as pltpu

F32 = jnp.float32
BF16 = jnp.bfloat16

D_MODEL = 2048
HEAD_DIM = 128
SWA_HEADS = 8
SWA_KV_HEADS = 2
SWA_GROUP = SWA_HEADS // SWA_KV_HEADS
WINDOW = 128
BLOCK = 128
MLA_HEADS = 8
QK_NOPE = 128
QK_ROPE = 64
V_DIM = 128
Q_LORA = 512
KV_LORA = 512
ROPE_THETA = 10000.0
SWA_WIDTH = SWA_HEADS * HEAD_DIM
SWA_KV_WIDTH = SWA_KV_HEADS * HEAD_DIM
MLA_WIDTH = MLA_HEADS * V_DIM
N_EXPERTS = 16
D_EXPERT = 1408
CAPACITY_FACTOR = 2
EPS = 1e-6
LANES = 128

SRC_QA = 0
SRC_KA = SRC_QA + SWA_WIDTH
SRC_VA = SRC_KA + SWA_KV_WIDTH
SRC_CQ = SRC_VA + SWA_KV_WIDTH
SRC_KR = SRC_CQ + Q_LORA + KV_LORA
OFF_KA = 0
OFF_CQ = OFF_KA + SWA_KV_WIDTH
OFF_CKV = OFF_CQ + Q_LORA
OFF_KR = OFF_CKV + KV_LORA
OFF_KRR = OFF_KR + 128
IN_EXT = OFF_KRR + 128
TOFF_QA = 0
TOFF_VA = TOFF_QA + SWA_WIDTH
IN_T = TOFF_VA + SWA_KV_WIDTH
MLA_QK = 256

SWA_SCALE = HEAD_DIM ** -0.5
MLA_SCALE = (QK_NOPE + QK_ROPE) ** -0.5 * float(np.log2(np.e))
NEG_BIG = -1e30
MLA_TQ = 512
MLA_TK = 512

V7X_VMEM_BYTES = 64 * 1024 * 1024
VMEM_LIMIT = V7X_VMEM_BYTES - 8 * 1024 * 1024


def _cparams(*sem):
    return pltpu.CompilerParams(dimension_semantics=sem, vmem_limit_bytes=VMEM_LIMIT)


def _const_spec(shape):
    nd = len(shape)
    return pl.BlockSpec(shape, lambda *_: (0,) * nd, pipeline_mode=pl.Buffered(1))


def _rms(x, g):
    return x * lax.rsqrt(jnp.mean(x * x, axis=-1, keepdims=True) + EPS) * g


def _dot(a, b):
    return jnp.dot(a, b, preferred_element_type=F32)


def _dot_nt(a, b):
    return lax.dot_general(a, b, (((1,), (1,)), ((), ())), preferred_element_type=F32)


def _in_proj_kernel(x_ref, g_ref, win_ref, wint_ref, gcq_ref, wqmt_ref, wqrt_ref, gckv_ref, wuk_ref,
                    wuvt_ref, cos_ref, sin_ref, cost_ref, sint_ref,
                    qat_ref, ka_ref, vat_ref, qmt_ref, km_ref, vt_ref):
    h = _rms(x_ref[...], g_ref[...]).astype(BF16)
    proj = _dot(h, win_ref[...])
    proj_t = _dot_nt(wint_ref[...], h)
    qat_ref[...] = (proj_t[TOFF_QA:TOFF_VA] * SWA_SCALE).astype(BF16)
    vat_ref[...] = proj_t[TOFF_VA:IN_T].astype(BF16)
    ka_ref[...] = proj[:, OFF_KA:OFF_CQ].astype(BF16)
    cos = cos_ref[...]
    sin = sin_ref[...]
    krope = (proj[:, OFF_KR:OFF_KRR] * cos + proj[:, OFF_KRR:IN_EXT] * sin).astype(BF16)
    cqn = _rms(proj[:, OFF_CQ:OFF_CKV], gcq_ref[...]).astype(BF16)
    qmain = _dot_nt(wqmt_ref[...], cqn)
    qrot = _dot_nt(wqrt_ref[...], cqn)
    cos_t = cost_ref[...]
    sin_t = sint_ref[...]
    for hh in range(MLA_HEADS):
        lo = hh * MLA_QK
        qmt_ref[lo:lo + QK_NOPE, :] = (qmain[lo:lo + QK_NOPE] * MLA_SCALE).astype(BF16)
        rp = qmain[lo + QK_NOPE:lo + MLA_QK] * cos_t + qrot[hh * 128:(hh + 1) * 128] * sin_t
        qmt_ref[lo + QK_NOPE:lo + MLA_QK, :] = (rp * MLA_SCALE).astype(BF16)
    ckvn = _rms(proj[:, OFF_CKV:OFF_KR], gckv_ref[...]).astype(BF16)
    kn = _dot(ckvn, wuk_ref[...])
    for hh in range(MLA_HEADS):
        lo = hh * MLA_QK
        km_ref[:, lo:lo + QK_NOPE] = kn[:, hh * QK_NOPE:(hh + 1) * QK_NOPE].astype(BF16)
        km_ref[:, lo + QK_NOPE:lo + MLA_QK] = krope
    vt_ref[...] = _dot_nt(wuvt_ref[...], ckvn).astype(BF16)


def _in_proj(x, lw, tables, tm):
    b, s, d = x.shape
    row = lambda w: pl.BlockSpec((None, tm, w), lambda i, j: (i, j, 0))
    tab = pl.BlockSpec((tm, 128), lambda i, j: (j, 0))
    tab_t = pl.BlockSpec((128, tm), lambda i, j: (0, j))
    col = lambda w: pl.BlockSpec((None, w, tm), lambda i, j: (i, 0, j))
    out_shapes = (
        jax.ShapeDtypeStruct((b, SWA_WIDTH, s), BF16),
        jax.ShapeDtypeStruct((b, s, SWA_KV_WIDTH), BF16),
        jax.ShapeDtypeStruct((b, SWA_KV_WIDTH, s), BF16),
        jax.ShapeDtypeStruct((b, MLA_HEADS * MLA_QK, s), BF16),
        jax.ShapeDtypeStruct((b, s, MLA_HEADS * MLA_QK), BF16),
        jax.ShapeDtypeStruct((b, s // tm, MLA_WIDTH, tm), BF16),
    )
    return pl.pallas_call(
        _in_proj_kernel,
        grid=(b, s // tm),
        in_specs=[
            row(d),
            _const_spec((1, d)),
            _const_spec((d, IN_EXT)),
            _const_spec((IN_T, d)),
            _const_spec((1, Q_LORA)),
            _const_spec((MLA_HEADS * MLA_QK, Q_LORA)),
            _const_spec((MLA_HEADS * 128, Q_LORA)),
            _const_spec((1, KV_LORA)),
            _const_spec((KV_LORA, MLA_HEADS * QK_NOPE)),
            _const_spec((MLA_WIDTH, KV_LORA)),
            tab, tab, tab_t, tab_t,
        ],
        out_specs=(col(SWA_WIDTH), row(SWA_KV_WIDTH), col(SWA_KV_WIDTH), col(MLA_HEADS * MLA_QK),
                   row(MLA_HEADS * MLA_QK),
                   pl.BlockSpec((None, None, MLA_WIDTH, tm), lambda i, j: (i, j, 0, 0))),
        out_shape=out_shapes,
        compiler_params=_cparams("parallel", "parallel"),
    )(x, lw["g_attn"], lw["w_in"], lw["w_in_t"], lw["g_cq"], lw["w_uq_main_t"], lw["w_uq_rot_t"], lw["g_ckv"],
      lw["w_uk"], lw["w_uvt"], *tables)


def _alibi_slopes():
    return [float(np.float32(2.0 ** (-8.0 * (i + 1) / SWA_HEADS))) for i in range(SWA_HEADS)]


SWA_TQ = 512


def _swa_kernel(sink_ref, qt_ref, kp_ref, kc_ref, kn_ref, vtp_ref, vtc_ref, vtn_ref, o_ref, *, seq):
    tq = qt_ref.shape[1]
    base = pl.program_id(1) * tq
    ki = lax.broadcasted_iota(jnp.int32, (3 * BLOCK, BLOCK), 0)
    qj = lax.broadcasted_iota(jnp.int32, (3 * BLOCK, BLOCK), 1)
    rel = ki - BLOCK - qj
    in_band = jnp.abs(rel) <= WINDOW
    dist = jnp.abs(rel).astype(F32)
    slopes = _alibi_slopes()
    kall = jnp.concatenate([kp_ref[...], kc_ref[...], kn_ref[...]], axis=0)
    vtall = jnp.concatenate([vtp_ref[...], vtc_ref[...], vtn_ref[...]], axis=1)
    for sb in range(tq // BLOCK):
        spos = base + sb * BLOCK + ki - BLOCK
        valid = in_band & (spos >= 0) & (spos < seq)
        cols = slice(sb * BLOCK, (sb + 1) * BLOCK)
        win = slice(sb * BLOCK, (sb + 3) * BLOCK)
        for kvh in range(SWA_KV_HEADS):
            kh = kall[win, kvh * HEAD_DIM:(kvh + 1) * HEAD_DIM]
            vth = vtall[kvh * HEAD_DIM:(kvh + 1) * HEAD_DIM, win]
            heads = [kvh * SWA_GROUP + g for g in range(SWA_GROUP)]
            qt = jnp.concatenate([qt_ref[hd * HEAD_DIM:(hd + 1) * HEAD_DIM, cols] for hd in heads], axis=1)
            raw = _dot(kh, qt)
            logits = jnp.concatenate(
                [jnp.where(valid, raw[:, g * BLOCK:(g + 1) * BLOCK] - slopes[hd] * dist, NEG_BIG)
                 for g, hd in enumerate(heads)], axis=1)
            sk = jnp.concatenate([jnp.full((1, BLOCK), sink_ref[hd], F32) for hd in heads], axis=1)
            m = jnp.maximum(jnp.max(logits, axis=0, keepdims=True), sk)
            p = jnp.exp(logits - m)
            denom = jnp.sum(p, axis=0, keepdims=True) + jnp.exp(sk - m)
            ot = _dot(vth, p.astype(BF16)) / denom
            for g, hd in enumerate(heads):
                o_ref[cols, hd * HEAD_DIM:(hd + 1) * HEAD_DIM] = ot[:, g * BLOCK:(g + 1) * BLOCK].T.astype(BF16)


def _swa(qat, ka, vat, sink):
    b, s, _ = ka.shape
    tq = _pick(s, SWA_TQ)
    r = tq // BLOCK
    nb = s // BLOCK
    prev = lambda j: jnp.maximum(j * r - 1, 0)
    nxt = lambda j: jnp.minimum(j * r + r, nb - 1)
    kedge = lambda f: pl.BlockSpec((None, BLOCK, SWA_KV_WIDTH), lambda i, j: (i, f(j), 0))
    vedge = lambda f: pl.BlockSpec((None, SWA_KV_WIDTH, BLOCK), lambda i, j: (i, 0, f(j)))
    return pl.pallas_call(
        functools.partial(_swa_kernel, seq=s),
        grid=(b, s // tq),
        in_specs=[pl.BlockSpec(memory_space=pltpu.SMEM),
                  pl.BlockSpec((None, SWA_WIDTH, tq), lambda i, j: (i, 0, j)),
                  kedge(prev), pl.BlockSpec((None, tq, SWA_KV_WIDTH), lambda i, j: (i, j, 0)), kedge(nxt),
                  vedge(prev), pl.BlockSpec((None, SWA_KV_WIDTH, tq), lambda i, j: (i, 0, j)), vedge(nxt)],
        out_specs=pl.BlockSpec((None, tq, SWA_WIDTH), lambda i, j: (i, j, 0)),
        out_shape=jax.ShapeDtypeStruct((b, s, SWA_WIDTH), BF16),
        compiler_params=_cparams("parallel", "parallel"),
    )(sink, qat, ka, ka, ka, vat, vat, vat)


MLA_NSUB = 2
MLA_SUM_ROWS = 16


def _mla_kernel(q_ref, k_ref, vt_ref, *rest, cast):
    if cast:
        g_ref, u_ref, d_ref, o_ref, wgu_ref, wd_ref, st_ref = rest
        wgu_ref[:, :D_EXPERT] = g_ref[...].astype(BF16)
        wgu_ref[:, D_EXPERT:] = u_ref[...].astype(BF16)
        wd_ref[...] = d_ref[...].astype(BF16)
    else:
        o_ref, st_ref = rest
    nsub, _, tk, tq = st_ref.shape
    nchunk = vt_ref.shape[0]
    assert nchunk % 2 == 0
    qs = [q_ref[:, t * tq:(t + 1) * tq] for t in range(nsub)]

    def produce(t, slot, c):
        k0 = pl.multiple_of(c * tk, tk)
        st = _dot(k_ref[pl.ds(k0, tk), :], qs[t])
        st_ref[t, slot] = st
        return jnp.max(st, axis=0, keepdims=True)

    ones_rows = jnp.ones((MLA_SUM_ROWS, tk), BF16)

    def consume(t, slot, c, mx, state):
        m, acc = state
        m_new = jnp.maximum(m, mx)
        alpha = jnp.exp2(m - m_new)
        p = jnp.exp2(st_ref[t, slot] - m_new)
        vext = jnp.concatenate([vt_ref[c], ones_rows], axis=0)
        acc = alpha * acc + _dot(vext, p.astype(BF16))
        return m_new, acc

    def pair(c, mx0, states, produce_next):
        mx1 = [produce(t, 1, c + 1) for t in range(nsub)]
        states = [consume(t, 0, c, mx0[t], states[t]) for t in range(nsub)]
        if produce_next:
            mx0 = [produce(t, 0, c + 2) for t in range(nsub)]
        states = [consume(t, 1, c + 1, mx1[t], states[t]) for t in range(nsub)]
        return mx0, states

    def body(i, carry):
        return pair(2 * i, *carry, produce_next=True)

    state0 = (jnp.full((1, tq), NEG_BIG, F32), jnp.zeros((V_DIM + MLA_SUM_ROWS, tq), F32))
    init = ([produce(t, 0, 0) for t in range(nsub)], [state0] * nsub)
    mx0, states = lax.fori_loop(0, nchunk // 2 - 1, body, init, unroll=2)
    _, states = pair(nchunk - 2, mx0, states, produce_next=False)
    for t in range(nsub):
        acc = states[t][1]
        o_ref[t * tq:(t + 1) * tq, :] = (acc[:V_DIM] / acc[V_DIM:V_DIM + 1]).T.astype(BF16)


def _cast_plan(steps, e, d, f):
    if (e * d) % steps:
        return None
    rows_gu = e * d // steps
    rows_d = [r for r in range(BF16_SUBLANES, f + 1, BF16_SUBLANES) if f % r == 0 and e * (f // r) <= steps]
    if rows_gu % BF16_SUBLANES or d % rows_gu or not rows_d:
        return None
    return rows_gu, rows_d[0]


def _mla(qmt, km, vt, raw=None, layer=0):
    b, s, _ = km.shape
    nchunk, tk = vt.shape[1], vt.shape[3]
    tq = _pick(s, MLA_TQ)
    nsub = MLA_NSUB if s % (MLA_NSUB * tq) == 0 else 1
    nq = s // (nsub * tq)
    in_specs = [pl.BlockSpec((None, MLA_QK, nsub * tq), lambda bb, h, i: (bb, h, i)),
                pl.BlockSpec((None, s, MLA_QK), lambda bb, h, i: (bb, 0, h)),
                pl.BlockSpec((None, nchunk, V_DIM, tk), lambda bb, h, i: (bb, 0, h, 0))]
    out_specs = [pl.BlockSpec((None, nsub * tq, V_DIM), lambda bb, h, i: (bb, i, h))]
    out_shape = [jax.ShapeDtypeStruct((b, s, MLA_WIDTH), BF16)]
    args = [qmt, km, vt]
    if raw is not None:
        w_gate, w_up, w_down = raw
        _, e, d, f = w_gate.shape
        rows_gu, rows_d = _cast_plan(b * MLA_HEADS * nq, e, d, f)
        step = lambda bb, h, i: (bb * MLA_HEADS + h) * nq + i
        per_e = d // rows_gu
        gu_idx = lambda bb, h, i: (step(bb, h, i) // per_e, step(bb, h, i) % per_e, 0)
        nd = f // rows_d
        d_step = lambda bb, h, i: jnp.minimum(step(bb, h, i), e * nd - 1)
        d_idx = lambda bb, h, i: (d_step(bb, h, i) // nd, d_step(bb, h, i) % nd, 0)
        in_specs += [pl.BlockSpec((None, None, rows_gu, f), lambda *a: (layer,) + gu_idx(*a)),
                     pl.BlockSpec((None, None, rows_gu, f), lambda *a: (layer,) + gu_idx(*a)),
                     pl.BlockSpec((None, None, rows_d, d), lambda *a: (layer,) + d_idx(*a))]
        out_specs += [pl.BlockSpec((None, rows_gu, 2 * f), gu_idx), pl.BlockSpec((None, rows_d, d), d_idx)]
        out_shape += [jax.ShapeDtypeStruct((e, d, 2 * f), BF16), jax.ShapeDtypeStruct((e, f, d), BF16)]
        args += [w_gate, w_up, w_down]
    outs = pl.pallas_call(
        functools.partial(_mla_kernel, cast=raw is not None),
        grid=(b, MLA_HEADS, nq),
        in_specs=in_specs,
        out_specs=out_specs,
        out_shape=out_shape,
        scratch_shapes=[pltpu.VMEM((nsub, 2, tk, tq), F32)],
        compiler_params=_cparams("arbitrary", "arbitrary", "arbitrary"),
    )(*args)
    return outs[0] if raw is None else outs


H2_ROWS = D_MODEL // 2 // LANES


def _out_proj_kernel(x_ref, ya_ref, yb_ref, gya_ref, gyb_ref, woa_ref, wob_ref, gffn_ref, wrt_ref,
                     xo_ref, h2_ref, afft_ref):
    yan = _rms(ya_ref[...].astype(F32), gya_ref[...]).astype(BF16)
    ybn = _rms(yb_ref[...].astype(F32), gyb_ref[...]).astype(BF16)
    xn = x_ref[...] + _dot(yan, woa_ref[...]) + _dot(ybn, wob_ref[...])
    xo_ref[...] = xn
    h2 = _rms(xn, gffn_ref[...]).astype(BF16)
    half = h2.shape[1] // 2
    hi = pltpu.bitcast(h2[:, :half].astype(F32), jnp.uint32)
    lo = pltpu.bitcast(h2[:, half:].astype(F32), jnp.uint32)
    packed = hi | (lo >> 16)
    tm = packed.shape[0]
    for c in range(H2_ROWS):
        h2_ref[pl.ds(c, tm, stride=H2_ROWS), :] = packed[:, c * LANES:(c + 1) * LANES]
    logits = _dot_nt(wrt_ref[...], h2)
    m = jnp.max(logits, axis=0, keepdims=True)
    p = jnp.exp(logits - m)
    afft_ref[...] = p / jnp.sum(p, axis=0, keepdims=True)


def _out_proj(x2, ya2, yb2, lw, tm):
    n, d = x2.shape
    row = lambda w: pl.BlockSpec((tm, w), lambda i: (i, 0))
    return pl.pallas_call(
        _out_proj_kernel,
        grid=(n // tm,),
        in_specs=[row(d), row(SWA_WIDTH), row(MLA_WIDTH),
                  _const_spec((1, SWA_WIDTH)), _const_spec((1, MLA_WIDTH)),
                  _const_spec((SWA_WIDTH, d)), _const_spec((MLA_WIDTH, d)),
                  _const_spec((1, d)), _const_spec((N_EXPERTS, d))],
        out_specs=(row(d), pl.BlockSpec((tm * H2_ROWS, LANES), lambda i: (i, 0)),
                   pl.BlockSpec((N_EXPERTS, tm), lambda i: (0, i))),
        out_shape=(jax.ShapeDtypeStruct((n, d), F32), jax.ShapeDtypeStruct((n * H2_ROWS, LANES), jnp.uint32),
                   jax.ShapeDtypeStruct((N_EXPERTS, n), F32)),
        compiler_params=_cparams("parallel"),
    )(x2, ya2, yb2, lw["g_ya"], lw["g_yb"], lw["w_out_a"], lw["w_out_b"], lw["g_ffn"], lw["w_router_t"])


SEL_CJ = 1024


def _token_cumsum(mask, upper, ones, lstrict):
    mb = mask.astype(BF16)
    within = _dot(mb, upper)
    tot = _dot(mb, ones)
    rowpre = _dot(lstrict, tot.astype(BF16))
    return within, tot, rowpre


def _select_kernel(a_ref, idx_ref, gate_ref, pos_ref, rowpre_ref, *, cap):
    a = a_ref[...]
    g = a.shape[0]
    bits = pltpu.bitcast(a, jnp.int32)

    def count_ge(cand):
        return jnp.sum(jnp.where(bits >= cand, 1.0, 0.0), keepdims=True)

    def radix_step(i, t):
        hi = jnp.left_shift(jnp.int32(1), 29 - 2 * i)
        lo = jnp.left_shift(jnp.int32(1), 28 - 2 * i)
        c3, c2, c1 = t | hi | lo, t | hi, t | lo
        return jnp.where(count_ge(c3) >= cap, c3,
                         jnp.where(count_ge(c2) >= cap, c2, jnp.where(count_ge(c1) >= cap, c1, t)))

    top = jnp.full((1, 1), 1 << 30, jnp.int32)
    thr = jnp.where(count_ge(top) >= cap, top, jnp.zeros((1, 1), jnp.int32))
    thr = lax.fori_loop(0, 15, radix_step, thr)
    ii = lax.broadcasted_iota(jnp.int32, (LANES, LANES), 0)
    jj = lax.broadcasted_iota(jnp.int32, (LANES, LANES), 1)
    upper = jnp.where(ii <= jj, 1.0, 0.0).astype(BF16)
    ones = jnp.ones((LANES, LANES), BF16)
    gi = lax.broadcasted_iota(jnp.int32, (g, g), 0)
    gj = lax.broadcasted_iota(jnp.int32, (g, g), 1)
    lstrict = jnp.where(gj < gi, 1.0, 0.0).astype(BF16)

    gt = bits > thr
    eq = bits == thr
    need = cap - jnp.sum(jnp.where(gt, 1.0, 0.0), keepdims=True)
    eqf = jnp.where(eq, 1.0, 0.0)
    w_eq, _, rp_eq = _token_cumsum(eqf, upper, ones, lstrict)
    tie_rank = w_eq - eqf + rp_eq
    self_ = jnp.where(gt, 1.0, jnp.where(tie_rank < need, eqf, 0.0))
    w, tot, rp = _token_cumsum(self_, upper, ones, lstrict)
    pos_ref[...] = jnp.where(self_ > 0.0, w - 1.0 + rp, -1.0)
    rpt = rp.T
    rowpre_ref[...] = rpt[0:1, :]

    rc = rp + tot
    wt = w.T.astype(BF16)
    pre_hi = jnp.floor(rpt[0:16] * (1.0 / 64.0))
    pre_lo = rpt[0:16] - 64.0 * pre_hi
    pre_hi = pre_hi.astype(BF16)
    pre_lo = pre_lo.astype(BF16)
    at = a.T
    a_h = at.astype(BF16)
    r1 = at - a_h.astype(F32)
    a_m = r1.astype(BF16)
    a_l = (r1 - a_m.astype(F32)).astype(BF16)
    cj_n = min(SEL_CJ, cap)
    for jc in range(cap // cj_n):
        j = (lax.broadcasted_iota(jnp.int32, (1, cj_n), 1) + jc * cj_n).astype(F32)
        rct = jnp.tile(rc, (1, cj_n // LANES))
        row = jnp.sum(jnp.where(rct <= j, 1.0, 0.0), axis=0, keepdims=True)
        grow = lax.broadcasted_iota(jnp.int32, (g, cj_n), 0).astype(F32)
        onehot = jnp.where(grow == row, 1.0, 0.0).astype(BF16)
        wg = _dot(wt, onehot)
        pre = 64.0 * _dot(pre_hi, onehot)[0:1] + _dot(pre_lo, onehot)[0:1]
        jloc = j - pre
        lane = jnp.sum(jnp.where(wg <= jloc, 1.0, 0.0), axis=0, keepdims=True)
        idx_ref[:, jc * cj_n:(jc + 1) * cj_n] = (row * LANES + lane).astype(jnp.int32)
        ag = _dot(a_h, onehot) + _dot(a_m, onehot) + _dot(a_l, onehot)
        lrow = lax.broadcasted_iota(jnp.int32, (LANES, cj_n), 0).astype(F32)
        gate_ref[:, jc * cj_n:(jc + 1) * cj_n] = jnp.sum(jnp.where(lrow == lane, ag, 0.0), axis=0,
                                                          keepdims=True)


def _select(afft, cap):
    e, n = afft.shape
    g = n // LANES
    blk = lambda w: pl.BlockSpec((None, 1, w), lambda i: (i, 0, 0))
    tok = pl.BlockSpec((None, g, LANES), lambda i: (i, 0, 0))
    return pl.pallas_call(
        functools.partial(_select_kernel, cap=cap),
        grid=(e,),
        in_specs=[tok],
        out_specs=(blk(cap), blk(cap), tok, blk(g)),
        out_shape=(jax.ShapeDtypeStruct((e, 1, cap), jnp.int32), jax.ShapeDtypeStruct((e, 1, cap), F32),
                   jax.ShapeDtypeStruct((e, g, LANES), F32), jax.ShapeDtypeStruct((e, 1, g), F32)),
        compiler_params=_cparams("parallel"),
    )(afft.reshape(e, g, LANES))


FFN_TT = 256
FFN_SPLIT = 2
FFN_SLOTS = 3


def _ffn_kernel(idx_ref, h2_hbm, wgu_ref, wd_ref, gate_ref, ye_ref, xbuf, sem):
    tt = xbuf.shape[1] // H2_ROWS
    nj = pl.num_programs(1)
    step = pl.program_id(0) * nj + pl.program_id(1)
    last = pl.num_programs(0) * nj - 1
    nslot = xbuf.shape[0]
    ahead = nslot - 1
    slot = step % nslot

    def row_copy(tok, sl, r):
        return pltpu.make_async_copy(h2_hbm.at[pl.ds(pl.multiple_of(tok * H2_ROWS, H2_ROWS), H2_ROWS), :],
                                     xbuf.at[sl, pl.ds(r * H2_ROWS, H2_ROWS), :], sem.at[sl])

    def issue(s):
        sl = s % nslot
        for r in range(tt):
            row_copy(idx_ref[s * tt + r], sl, r).start()

    def wait_slot(sl):
        pltpu.make_async_copy(h2_hbm.at[pl.ds(0, tt * H2_ROWS), :], xbuf.at[sl], sem.at[sl]).wait()

    @pl.when(step == 0)
    def _():
        for s in range(ahead):
            @pl.when(s <= last)
            def _():
                issue(s)

    @pl.when(step + ahead <= last)
    def _():
        issue(step + ahead)

    wait_slot(slot)
    words = [xbuf[slot, pl.ds(c, tt, stride=H2_ROWS), :] for c in range(H2_ROWS)]
    x = jnp.concatenate([pltpu.bitcast(w & jnp.uint32(0xFFFF0000), F32).astype(BF16) for w in words]
                        + [pltpu.bitcast(w << 16, F32).astype(BF16) for w in words], axis=1)
    for part in range(FFN_SPLIT):
        r = slice(part * (tt // FFN_SPLIT), (part + 1) * (tt // FFN_SPLIT))
        au = _dot(x[r], wgu_ref[...])
        a = au[:, :D_EXPERT]
        u = au[:, D_EXPERT:]
        act = (a / (1.0 + jnp.exp(-a)) * u).astype(BF16)
        ye_ref[r, :] = (_dot(act, wd_ref[...]) * gate_ref[r, :]).astype(BF16)


def _ffn(idx_flat, h2p, wgu, wd, gate, cap):
    e, d = wgu.shape[0], wgu.shape[1]
    tt = _pick(cap, FFN_TT)
    return pl.pallas_call(
        _ffn_kernel,
        grid_spec=pltpu.PrefetchScalarGridSpec(
            num_scalar_prefetch=1,
            grid=(e, cap // tt),
            in_specs=[pl.BlockSpec(memory_space=pl.ANY),
                      pl.BlockSpec((None, d, 2 * D_EXPERT), lambda i, j, idx: (i, 0, 0)),
                      pl.BlockSpec((None, D_EXPERT, d), lambda i, j, idx: (i, 0, 0)),
                      pl.BlockSpec((None, tt, 1), lambda i, j, idx: (i, j, 0))],
            out_specs=pl.BlockSpec((None, tt, d), lambda i, j, idx: (i, j, 0)),
            scratch_shapes=[pltpu.VMEM((FFN_SLOTS, tt * H2_ROWS, LANES), jnp.uint32),
                            pltpu.SemaphoreType.DMA((FFN_SLOTS,))]),
        out_shape=jax.ShapeDtypeStruct((e, cap, d), BF16),
        compiler_params=_cparams("arbitrary", "arbitrary"),
    )(idx_flat, h2p, wgu, wd, gate)


CAST_ROWS = 1024
BF16_SUBLANES = 16


def _row_block(n, limit):
    return max(r for r in range(BF16_SUBLANES, limit + 1, BF16_SUBLANES) if n % r == 0)


def _cast_gu_kernel(g_ref, u_ref, o_ref):
    o_ref[:, :D_EXPERT] = g_ref[...].astype(BF16)
    o_ref[:, D_EXPERT:] = u_ref[...].astype(BF16)


def _cast_kernel(w_ref, o_ref):
    o_ref[...] = w_ref[...].astype(BF16)


def _cast_experts(w_gate, w_up, w_down):
    l, e, d, f = w_gate.shape
    blk = lambda r, c: pl.BlockSpec((None, None, r, c), lambda a, b, i: (a, b, i, 0))
    rg = _row_block(d, CAST_ROWS // 2)
    wgu = pl.pallas_call(
        _cast_gu_kernel, grid=(l, e, d // rg),
        in_specs=[blk(rg, f), blk(rg, f)], out_specs=blk(rg, 2 * f),
        out_shape=jax.ShapeDtypeStruct((l, e, d, 2 * f), BF16),
        compiler_params=_cparams("parallel", "parallel", "parallel"),
    )(w_gate, w_up)
    rd = _row_block(f, CAST_ROWS)
    wd = pl.pallas_call(
        _cast_kernel, grid=(l, e, f // rd),
        in_specs=[blk(rd, d)], out_specs=blk(rd, d),
        out_shape=jax.ShapeDtypeStruct((l, e, f, d), BF16),
        compiler_params=_cparams("parallel", "parallel", "parallel"),
    )(w_down)
    return wgu, wd


CMB_TT = 128
CMB_BLK = 16
CMB_KC = 256
CMB_ROWS = N_EXPERTS * CMB_TT + N_EXPERTS * 2 * CMB_BLK + CMB_KC


def _combine_kernel(rowpre_ref, x_ref, pos_ref, gfin_ref, ye_hbm, o_ref, buf, sem, acc_ref, *, g, final):
    i = pl.program_id(0)
    last = pl.num_programs(0) - 1
    slot = i % 2

    def plan(tile):
        starts, nblks, offs = [], [], []
        off = jnp.int32(0)
        for e in range(N_EXPERTS):
            lo = rowpre_ref[e * (g + 1) + tile]
            hi = rowpre_ref[e * (g + 1) + tile + 1]
            start = (lo >> 4) << 4
            nblk = jnp.where(hi > lo, (hi - start + (CMB_BLK - 1)) >> 4, 0)
            starts.append(start)
            nblks.append(nblk)
            offs.append(off)
            off = off + nblk
        return starts, nblks, offs, off

    def block_copy(e, src_row, dst_row, sl):
        return pltpu.make_async_copy(ye_hbm.at[e, pl.ds(src_row, CMB_BLK), :],
                                     buf.at[sl, pl.ds(dst_row, CMB_BLK), :], sem.at[sl])

    def issue(tile, sl):
        starts, nblks, offs, _ = plan(tile)
        for e in range(N_EXPERTS):
            def body(b, c, e=e):
                block_copy(e, pl.multiple_of(starts[e] + b * CMB_BLK, CMB_BLK),
                           pl.multiple_of((offs[e] + b) * CMB_BLK, CMB_BLK), sl).start()
                return c
            lax.fori_loop(0, nblks[e], body, 0)

    def wait(tile, sl):
        rows = plan(tile)[3] * CMB_BLK

        @pl.when(rows > 0)
        def _():
            pltpu.make_async_copy(ye_hbm.at[0, pl.ds(0, rows), :], buf.at[sl, pl.ds(0, rows), :],
                                  sem.at[sl]).wait()

    @pl.when(i == 0)
    def _():
        buf[...] = jnp.zeros_like(buf)
        issue(0, 0)

    @pl.when(i < last)
    def _():
        issue(i + 1, 1 - slot)

    wait(i, slot)
    starts, nblks, offs, total = plan(i)
    pos = jnp.concatenate([pos_ref[...], jnp.zeros((LANES - N_EXPERTS, CMB_TT), F32)], axis=0).T
    cols = []
    for e in range(N_EXPERTS):
        pe = pos[:, e:e + 1]
        shift = (offs[e] * CMB_BLK - starts[e]).astype(F32)
        col = jnp.where(pe >= 0.0, pe + shift, -1.0)
        cols.append(jnp.broadcast_to(col, (CMB_TT, CMB_KC)))
    acc_ref[...] = x_ref[...]

    def chunk(k, c):
        k0 = pl.multiple_of(k * CMB_KC, CMB_KC)
        r = (lax.broadcasted_iota(jnp.int32, (CMB_TT, CMB_KC), 1) + k0).astype(F32)
        hit = jnp.where(cols[0] == r, 1.0, 0.0)
        for e in range(1, N_EXPERTS):
            hit = hit + jnp.where(cols[e] == r, 1.0, 0.0)
        acc_ref[...] += _dot(hit.astype(BF16), buf[slot, pl.ds(k0, CMB_KC), :])
        return c

    lax.fori_loop(0, (total * CMB_BLK + (CMB_KC - 1)) // CMB_KC, chunk, 0)
    o_ref[...] = _rms(acc_ref[...], gfin_ref[...]) if final else acc_ref[...]


def _combine(rowpre_flat, x2, pos, g_final, ye, g, final):
    n, d = x2.shape
    assert n == g * CMB_TT
    return pl.pallas_call(
        functools.partial(_combine_kernel, g=g, final=final),
        grid_spec=pltpu.PrefetchScalarGridSpec(
            num_scalar_prefetch=1,
            grid=(g,),
            in_specs=[pl.BlockSpec((CMB_TT, d), lambda i, rp: (i, 0)),
                      pl.BlockSpec((N_EXPERTS, CMB_TT), lambda i, rp: (0, i)),
                      pl.BlockSpec((1, d), lambda i, rp: (0, 0)),
                      pl.BlockSpec(memory_space=pl.ANY)],
            out_specs=pl.BlockSpec((CMB_TT, d), lambda i, rp: (i, 0)),
            scratch_shapes=[pltpu.VMEM((2, CMB_ROWS, d), BF16), pltpu.SemaphoreType.DMA((2,)),
                            pltpu.VMEM((CMB_TT, d), F32)]),
        out_shape=jax.ShapeDtypeStruct((n, d), F32),
        compiler_params=_cparams("arbitrary"),
    )(rowpre_flat, x2, pos, g_final, ye)


def _rot_half_cols(w):
    half = w.shape[-1] // 2
    return jnp.concatenate([-w[..., half:], w[..., :half]], axis=-1)


def _prep_layer(l, g_attn, w_in, sink, g_cq, w_uq, g_ckv, w_uk, w_uv, g_ya, g_yb, w_out, g_ffn,
                w_router):
    wi = w_in[l]
    w_kr = wi[:, SRC_KR:SRC_KR + QK_ROPE]
    zpad = jnp.zeros((D_MODEL, 128 - QK_ROPE), F32)
    w_in_ext = jnp.concatenate([wi[:, SRC_KA:SRC_VA], wi[:, SRC_CQ:SRC_KR], w_kr, zpad,
                                _rot_half_cols(w_kr), zpad], axis=1)
    w_in_t = jnp.concatenate([wi[:, SRC_QA:SRC_KA], wi[:, SRC_VA:SRC_CQ]], axis=1).T
    wq = w_uq[l].reshape(Q_LORA, MLA_HEADS, QK_NOPE + QK_ROPE)
    wq_rope = wq[:, :, QK_NOPE:]
    zq = jnp.zeros((Q_LORA, MLA_HEADS, 128 - QK_ROPE), F32)
    w_uq_main = jnp.concatenate([wq, zq], axis=-1).reshape(Q_LORA, MLA_HEADS * MLA_QK)
    w_uq_rot = jnp.concatenate([_rot_half_cols(wq_rope), zq], axis=-1).reshape(Q_LORA, MLA_HEADS * 128)
    return dict(
        g_attn=g_attn[l][None], w_in=w_in_ext.astype(BF16), w_in_t=w_in_t.astype(BF16), sink=sink[l],
        g_cq=g_cq[l][None], w_uq_main_t=w_uq_main.T.astype(BF16), w_uq_rot_t=w_uq_rot.T.astype(BF16),
        g_ckv=g_ckv[l][None], w_uk=w_uk[l].astype(BF16), w_uvt=w_uv[l].T.astype(BF16),
        g_ya=g_ya[l][None], g_yb=g_yb[l][None],
        w_out_a=w_out[l][:SWA_WIDTH].astype(BF16), w_out_b=w_out[l][SWA_WIDTH:].astype(BF16),
        g_ffn=g_ffn[l][None], w_router_t=w_router[l].T.astype(BF16), layer=l,
    )


def _rope_tables(s):
    inv = 1.0 / (ROPE_THETA ** (jnp.arange(0, QK_ROPE, 2, dtype=F32) / QK_ROPE))
    ang = jnp.arange(s, dtype=F32)[:, None] * inv[None, :]
    z = jnp.zeros((s, 128 - QK_ROPE), F32)
    cos, sin = jnp.cos(ang), jnp.sin(ang)
    cos128 = jnp.concatenate([cos, cos, z], axis=1)
    sin128 = jnp.concatenate([sin, sin, z], axis=1)
    return cos128, sin128, cos128.T, sin128.T


def _pick(n, pref):
    t = pref
    while n % t:
        t //= 2
    return t


def _layer(x, lw, experts, raw, tables, g_final, final):
    b, s, d = x.shape
    n = b * s
    qat, ka, vat, qmt, km, vt = _in_proj(x, lw, tables, _pick(s, MLA_TK))
    ya = _swa(qat, ka, vat, lw["sink"])
    if experts is None:
        yb, *experts = _mla(qmt, km, vt, raw, lw["layer"])
    else:
        yb = _mla(qmt, km, vt)
    x2, h2, afft = _out_proj(x.reshape(n, d), ya.reshape(n, SWA_WIDTH), yb.reshape(n, MLA_WIDTH), lw,
                             _pick(n, 512))
    cap = CAPACITY_FACTOR * n // N_EXPERTS
    g = n // LANES
    idx, gate, pos, rowpre = _select(afft, cap)
    ye = _ffn(idx.reshape(-1), h2, *experts, gate.reshape(N_EXPERTS, cap, 1), cap)
    bounds = jnp.concatenate([rowpre.reshape(N_EXPERTS, g), jnp.full((N_EXPERTS, 1), cap, F32)], axis=1)
    x2 = _combine(bounds.astype(jnp.int32).reshape(-1), x2, pos.reshape(N_EXPERTS, n), g_final, ye, g, final)
    return x2.reshape(b, s, d), experts


def _trunk(x, layers, experts, raw, g_final):
    b, s, d = x.shape
    tables = _rope_tables(s)
    produced = []
    for l, lw in enumerate(layers):
        x, ex = _layer(x, lw, None if experts is None else experts[l], raw, tables, g_final[None],
                       final=lw is layers[-1])
        produced.append(ex)
    return x, produced


def kernel(x_prompt, x_sample, g_attn, w_in, sink, g_cq, w_uq, g_ckv, w_uk, w_uv, g_ya, g_yb, w_out,
           g_ffn, w_router, w_gate, w_up, w_down, g_final):
    depth = w_in.shape[0]
    layers = [_prep_layer(l, g_attn, w_in, sink, g_cq, w_uq, g_ckv, w_uk, w_uv, g_ya, g_yb, w_out,
                          g_ffn, w_router) for l in range(depth)]
    raw = (w_gate, w_up, w_down)
    b, s, _ = x_prompt.shape
    steps = b * MLA_HEADS * (s // (MLA_NSUB * MLA_TQ)) if s % (MLA_NSUB * MLA_TQ) == 0 else 0
    experts = None
    if not steps or _cast_plan(steps, *w_gate.shape[1:]) is None:
        wgu, wd = _cast_experts(*raw)
        experts = [(wgu[l], wd[l]) for l in range(depth)]
    y_prompt, experts = _trunk(x_prompt, layers, experts, raw, g_final)
    y_sample, _ = _trunk(x_sample, layers, experts, raw, g_final)
    return y_prompt, y_sample
```

```python
import functools

import numpy as np
import jax
import jax.numpy as jnp
from jax import lax
from jax.experimental import pallas as pl
from jax.experimental.pallas import tpu as pltpu

F32 = jnp.float32
BF16 = jnp.bfloat16

D_MODEL = 2048
HEAD_DIM = 128
SWA_HEADS = 8
SWA_KV_HEADS = 2
SWA_GROUP = SWA_HEADS // SWA_KV_HEADS
WINDOW = 128
BLOCK = 128
MLA_HEADS = 8
QK_NOPE = 128
QK_ROPE = 64
V_DIM = 128
Q_LORA = 512
KV_LORA = 512
ROPE_THETA = 10000.0
SWA_WIDTH = SWA_HEADS * HEAD_DIM
SWA_KV_WIDTH = SWA_KV_HEADS * HEAD_DIM
MLA_WIDTH = MLA_HEADS * V_DIM
N_EXPERTS = 16
D_EXPERT = 1408
CAPACITY_FACTOR = 2
EPS = 1e-6
LANES = 128

SRC_QA = 0
SRC_KA = SRC_QA + SWA_WIDTH
SRC_VA = SRC_KA + SWA_KV_WIDTH
SRC_CQ = SRC_VA + SWA_KV_WIDTH
SRC_KR = SRC_CQ + Q_LORA + KV_LORA
OFF_KA = 0
OFF_CQ = OFF_KA + SWA_KV_WIDTH
OFF_CKV = OFF_CQ + Q_LORA
OFF_KR = OFF_CKV + KV_LORA
OFF_KRR = OFF_KR + 128
IN_EXT = OFF_KRR + 128
TOFF_QA = 0
TOFF_VA = TOFF_QA + SWA_WIDTH
IN_T = TOFF_VA + SWA_KV_WIDTH
MLA_QK = 256

SWA_SCALE = HEAD_DIM ** -0.5
MLA_SCALE = (QK_NOPE + QK_ROPE) ** -0.5 * float(np.log2(np.e))
NEG_BIG = -1e30
MLA_TQ = 512
MLA_TK = 512

V7X_VMEM_BYTES = 64 * 1024 * 1024
VMEM_LIMIT = V7X_VMEM_BYTES - 8 * 1024 * 1024


def _cparams(*sem):
    return pltpu.CompilerParams(dimension_semantics=sem, vmem_limit_bytes=VMEM_LIMIT)


def _const_spec(shape):
    nd = len(shape)
    return pl.BlockSpec(shape, lambda *_: (0,) * nd, pipeline_mode=pl.Buffered(1))


def _rms(x, g):
    return x * lax.rsqrt(jnp.mean(x * x, axis=-1, keepdims=True) + EPS) * g


def _dot(a, b):
    return jnp.dot(a, b, preferred_element_type=F32)


def _dot_nt(a, b):
    return lax.dot_general(a, b, (((1,), (1,)), ((), ())), preferred_element_type=F32)


def _in_proj_kernel(x_ref, g_ref, win_ref, wint_ref, gcq_ref, wqmt_ref, wqrt_ref, gckv_ref, wuk_ref,
                    wuvt_ref, cos_ref, sin_ref, cost_ref, sint_ref,
                    qat_ref, ka_ref, vat_ref, qmt_ref, km_ref, vt_ref):
    h = _rms(x_ref[...], g_ref[...]).astype(BF16)
    proj = _dot(h, win_ref[...])
    proj_t = _dot_nt(wint_ref[...], h)
    qat_ref[...] = (proj_t[TOFF_QA:TOFF_VA] * SWA_SCALE).astype(BF16)
    vat_ref[...] = proj_t[TOFF_VA:IN_T].astype(BF16)
    ka_ref[...] = proj[:, OFF_KA:OFF_CQ].astype(BF16)
    cos = cos_ref[...]
    sin = sin_ref[...]
    krope = (proj[:, OFF_KR:OFF_KRR] * cos + proj[:, OFF_KRR:IN_EXT] * sin).astype(BF16)
    cqn = _rms(proj[:, OFF_CQ:OFF_CKV], gcq_ref[...]).astype(BF16)
    qmain = _dot_nt(wqmt_ref[...], cqn)
    qrot = _dot_nt(wqrt_ref[...], cqn)
    cos_t = cost_ref[0:QK_ROPE, :]
    sin_t = sint_ref[0:QK_ROPE, :]
    qk = QK_NOPE + QK_ROPE
    zero_rows = jnp.zeros((MLA_QK - qk, qmain.shape[1]), BF16)
    for hh in range(MLA_HEADS):
        lo = hh * MLA_QK
        qmt_ref[lo:lo + QK_NOPE, :] = (qmain[hh * qk:hh * qk + QK_NOPE] * MLA_SCALE).astype(BF16)
        rp = (qmain[hh * qk + QK_NOPE:(hh + 1) * qk] * cos_t
              + qrot[hh * QK_ROPE:(hh + 1) * QK_ROPE] * sin_t)
        qmt_ref[lo + QK_NOPE:lo + qk, :] = (rp * MLA_SCALE).astype(BF16)
        qmt_ref[lo + qk:lo + MLA_QK, :] = zero_rows
    ckvn = _rms(proj[:, OFF_CKV:OFF_KR], gckv_ref[...]).astype(BF16)
    kn = _dot(ckvn, wuk_ref[...])
    for hh in range(MLA_HEADS):
        lo = hh * MLA_QK
        km_ref[:, lo:lo + QK_NOPE] = kn[:, hh * QK_NOPE:(hh + 1) * QK_NOPE].astype(BF16)
        km_ref[:, lo + QK_NOPE:lo + MLA_QK] = krope
    vt_ref[...] = _dot_nt(wuvt_ref[...], ckvn).astype(BF16)


def _in_proj(x, lw, tables, tm):
    b, s, d = x.shape
    row = lambda w: pl.BlockSpec((None, tm, w), lambda i, j: (i, j, 0))
    tab = pl.BlockSpec((tm, 128), lambda i, j: (j, 0))
    tab_t = pl.BlockSpec((128, tm), lambda i, j: (0, j))
    col = lambda w: pl.BlockSpec((None, w, tm), lambda i, j: (i, 0, j))
    out_shapes = (
        jax.ShapeDtypeStruct((b, SWA_WIDTH, s), BF16),
        jax.ShapeDtypeStruct((b, s, SWA_KV_WIDTH), BF16),
        jax.ShapeDtypeStruct((b, SWA_KV_WIDTH, s), BF16),
        jax.ShapeDtypeStruct((b, MLA_HEADS * MLA_QK, s), BF16),
        jax.ShapeDtypeStruct((b, s, MLA_HEADS * MLA_QK), BF16),
        jax.ShapeDtypeStruct((b, s // tm, MLA_WIDTH, tm), BF16),
    )
    return pl.pallas_call(
        _in_proj_kernel,
        grid=(b, s // tm),
        in_specs=[
            row(d),
            _const_spec((1, d)),
            _const_spec((d, IN_EXT)),
            _const_spec((IN_T, d)),
            _const_spec((1, Q_LORA)),
            _const_spec((MLA_HEADS * (QK_NOPE + QK_ROPE), Q_LORA)),
            _const_spec((MLA_HEADS * QK_ROPE, Q_LORA)),
            _const_spec((1, KV_LORA)),
            _const_spec((KV_LORA, MLA_HEADS * QK_NOPE)),
            _const_spec((MLA_WIDTH, KV_LORA)),
            tab, tab, tab_t, tab_t,
        ],
        out_specs=(col(SWA_WIDTH), row(SWA_KV_WIDTH), col(SWA_KV_WIDTH), col(MLA_HEADS * MLA_QK),
                   row(MLA_HEADS * MLA_QK),
                   pl.BlockSpec((None, None, MLA_WIDTH, tm), lambda i, j: (i, j, 0, 0))),
        out_shape=out_shapes,
        compiler_params=_cparams("parallel", "parallel"),
    )(x, lw["g_attn"], lw["w_in"], lw["w_in_t"], lw["g_cq"], lw["w_uq_main_t"], lw["w_uq_rot_t"], lw["g_ckv"],
      lw["w_uk"], lw["w_uvt"], *tables)


def _alibi_slopes():
    return [float(np.float32(2.0 ** (-8.0 * (i + 1) / SWA_HEADS))) for i in range(SWA_HEADS)]


SWA_TQ = 1024


def _swa_kernel(sink_ref, qt_ref, kp_ref, kc_ref, kn_ref, vtp_ref, vtc_ref, vtn_ref, o_ref, *, seq):
    tq = qt_ref.shape[1]
    base = pl.program_id(1) * tq
    ki = lax.broadcasted_iota(jnp.int32, (3 * BLOCK, BLOCK), 0)
    qj = lax.broadcasted_iota(jnp.int32, (3 * BLOCK, BLOCK), 1)
    rel = ki - BLOCK - qj
    in_band = jnp.abs(rel) <= WINDOW
    dist = jnp.abs(rel).astype(F32)
    slopes = _alibi_slopes()
    kall = jnp.concatenate([kp_ref[...], kc_ref[...], kn_ref[...]], axis=0)
    vtall = jnp.concatenate([vtp_ref[...], vtc_ref[...], vtn_ref[...]], axis=1)
    for sb in range(tq // BLOCK):
        spos = base + sb * BLOCK + ki - BLOCK
        valid = in_band & (spos >= 0) & (spos < seq)
        cols = slice(sb * BLOCK, (sb + 1) * BLOCK)
        win = slice(sb * BLOCK, (sb + 3) * BLOCK)
        for kvh in range(SWA_KV_HEADS):
            kh = kall[win, kvh * HEAD_DIM:(kvh + 1) * HEAD_DIM]
            vth = vtall[kvh * HEAD_DIM:(kvh + 1) * HEAD_DIM, win]
            heads = [kvh * SWA_GROUP + g for g in range(SWA_GROUP)]
            qt = jnp.concatenate([qt_ref[hd * HEAD_DIM:(hd + 1) * HEAD_DIM, cols] for hd in heads], axis=1)
            raw = _dot(kh, qt)
            logits = jnp.concatenate(
                [jnp.where(valid, raw[:, g * BLOCK:(g + 1) * BLOCK] - slopes[hd] * dist, NEG_BIG)
                 for g, hd in enumerate(heads)], axis=1)
            sk = jnp.concatenate([jnp.full((1, BLOCK), sink_ref[hd], F32) for hd in heads], axis=1)
            m = jnp.maximum(jnp.max(logits, axis=0, keepdims=True), sk)
            p = jnp.exp(logits - m)
            denom = jnp.sum(p, axis=0, keepdims=True) + jnp.exp(sk - m)
            ot = _dot(vth, p.astype(BF16)) / denom
            for g, hd in enumerate(heads):
                o_ref[cols, hd * HEAD_DIM:(hd + 1) * HEAD_DIM] = ot[:, g * BLOCK:(g + 1) * BLOCK].T.astype(BF16)


def _swa(qat, ka, vat, sink):
    b, s, _ = ka.shape
    tq = _pick(s, SWA_TQ)
    r = tq // BLOCK
    nb = s // BLOCK
    prev = lambda j: jnp.maximum(j * r - 1, 0)
    nxt = lambda j: jnp.minimum(j * r + r, nb - 1)
    kedge = lambda f: pl.BlockSpec((None, BLOCK, SWA_KV_WIDTH), lambda i, j: (i, f(j), 0))
    vedge = lambda f: pl.BlockSpec((None, SWA_KV_WIDTH, BLOCK), lambda i, j: (i, 0, f(j)))
    return pl.pallas_call(
        functools.partial(_swa_kernel, seq=s),
        grid=(b, s // tq),
        in_specs=[pl.BlockSpec(memory_space=pltpu.SMEM),
                  pl.BlockSpec((None, SWA_WIDTH, tq), lambda i, j: (i, 0, j)),
                  kedge(prev), pl.BlockSpec((None, tq, SWA_KV_WIDTH), lambda i, j: (i, j, 0)), kedge(nxt),
                  vedge(prev), pl.BlockSpec((None, SWA_KV_WIDTH, tq), lambda i, j: (i, 0, j)), vedge(nxt)],
        out_specs=pl.BlockSpec((None, tq, SWA_WIDTH), lambda i, j: (i, j, 0)),
        out_shape=jax.ShapeDtypeStruct((b, s, SWA_WIDTH), BF16),
        compiler_params=_cparams("parallel", "parallel"),
    )(sink, qat, ka, ka, ka, vat, vat, vat)


MLA_NSUB = 2
MLA_SUM_ROWS = 16


def _mla_kernel(q_ref, k_ref, vt_ref, *rest, cast):
    if cast:
        g_ref, u_ref, d_ref, o_ref, wgu_ref, wd_ref, st_ref = rest
        wgu_ref[:, :D_EXPERT] = g_ref[...].astype(BF16)
        wgu_ref[:, D_EXPERT:] = u_ref[...].astype(BF16)
        wd_ref[...] = d_ref[...].astype(BF16)
    else:
        o_ref, st_ref = rest
    nsub, _, tk, tq = st_ref.shape
    nchunk = vt_ref.shape[0]
    assert nchunk % 2 == 0
    qs = [q_ref[:, t * tq:(t + 1) * tq] for t in range(nsub)]

    def produce(t, slot, c):
        k0 = pl.multiple_of(c * tk, tk)
        st = _dot(k_ref[pl.ds(k0, tk), :], qs[t])
        st_ref[t, slot] = st
        return jnp.max(st, axis=0, keepdims=True)

    ones_rows = jnp.ones((MLA_SUM_ROWS, tk), BF16)

    def consume(t, slot, c, mx, state):
        m, acc = state
        m_new = jnp.maximum(m, mx)
        alpha = jnp.exp2(m - m_new)
        p = jnp.exp2(st_ref[t, slot] - m_new)
        vext = jnp.concatenate([vt_ref[c], ones_rows], axis=0)
        acc = alpha * acc + _dot(vext, p.astype(BF16))
        return m_new, acc

    def pair(c, mx0, states, produce_next):
        mx1 = [produce(t, 1, c + 1) for t in range(nsub)]
        states = [consume(t, 0, c, mx0[t], states[t]) for t in range(nsub)]
        if produce_next:
            mx0 = [produce(t, 0, c + 2) for t in range(nsub)]
        states = [consume(t, 1, c + 1, mx1[t], states[t]) for t in range(nsub)]
        return mx0, states

    def body(i, carry):
        return pair(2 * i, *carry, produce_next=True)

    state0 = (jnp.full((1, tq), NEG_BIG, F32), jnp.zeros((V_DIM + MLA_SUM_ROWS, tq), F32))
    init = ([produce(t, 0, 0) for t in range(nsub)], [state0] * nsub)
    mx0, states = lax.fori_loop(0, nchunk // 2 - 1, body, init, unroll=2)
    _, states = pair(nchunk - 2, mx0, states, produce_next=False)
    for t in range(nsub):
        acc = states[t][1]
        o_ref[t * tq:(t + 1) * tq, :] = (acc[:V_DIM] / acc[V_DIM:V_DIM + 1]).T.astype(BF16)


def _cast_plan(steps, e, d, f):
    if (e * d) % steps:
        return None
    rows_gu = e * d // steps
    rows_d = [r for r in range(BF16_SUBLANES, f + 1, BF16_SUBLANES) if f % r == 0 and e * (f // r) <= steps]
    if rows_gu % BF16_SUBLANES or d % rows_gu or not rows_d:
        return None
    return rows_gu, rows_d[0]


def _mla(qmt, km, vt, raw=None, layer=0):
    b, s, _ = km.shape
    nchunk, tk = vt.shape[1], vt.shape[3]
    tq = _pick(s, MLA_TQ)
    nsub = MLA_NSUB if s % (MLA_NSUB * tq) == 0 else 1
    nq = s // (nsub * tq)
    in_specs = [pl.BlockSpec((None, MLA_QK, nsub * tq), lambda bb, h, i: (bb, h, i)),
                pl.BlockSpec((None, s, MLA_QK), lambda bb, h, i: (bb, 0, h)),
                pl.BlockSpec((None, nchunk, V_DIM, tk), lambda bb, h, i: (bb, 0, h, 0))]
    out_specs = [pl.BlockSpec((None, nsub * tq, V_DIM), lambda bb, h, i: (bb, i, h))]
    out_shape = [jax.ShapeDtypeStruct((b, s, MLA_WIDTH), BF16)]
    args = [qmt, km, vt]
    if raw is not None:
        w_gate, w_up, w_down = raw
        _, e, d, f = w_gate.shape
        rows_gu, rows_d = _cast_plan(b * MLA_HEADS * nq, e, d, f)
        step = lambda bb, h, i: (bb * MLA_HEADS + h) * nq + i
        per_e = d // rows_gu
        gu_idx = lambda bb, h, i: (step(bb, h, i) // per_e, step(bb, h, i) % per_e, 0)
        nd = f // rows_d
        d_step = lambda bb, h, i: jnp.minimum(step(bb, h, i), e * nd - 1)
        d_idx = lambda bb, h, i: (d_step(bb, h, i) // nd, d_step(bb, h, i) % nd, 0)
        in_specs += [pl.BlockSpec((None, None, rows_gu, f), lambda *a: (layer,) + gu_idx(*a)),
                     pl.BlockSpec((None, None, rows_gu, f), lambda *a: (layer,) + gu_idx(*a)),
                     pl.BlockSpec((None, None, rows_d, d), lambda *a: (layer,) + d_idx(*a))]
        out_specs += [pl.BlockSpec((None, rows_gu, 2 * f), gu_idx), pl.BlockSpec((None, rows_d, d), d_idx)]
        out_shape += [jax.ShapeDtypeStruct((e, d, 2 * f), BF16), jax.ShapeDtypeStruct((e, f, d), BF16)]
        args += [w_gate, w_up, w_down]
    outs = pl.pallas_call(
        functools.partial(_mla_kernel, cast=raw is not None),
        grid=(b, MLA_HEADS, nq),
        in_specs=in_specs,
        out_specs=out_specs,
        out_shape=out_shape,
        scratch_shapes=[pltpu.VMEM((nsub, 2, tk, tq), F32)],
        compiler_params=_cparams("arbitrary", "arbitrary", "arbitrary"),
    )(*args)
    return outs[0] if raw is None else outs


H2_ROWS = D_MODEL // 2 // LANES


def _out_proj_kernel(x_ref, ya_ref, yb_ref, gya_ref, gyb_ref, woa_ref, wob_ref, gffn_ref, wrt_ref,
                     xo_ref, h2_ref, afft_ref):
    yan = _rms(ya_ref[...].astype(F32), gya_ref[...]).astype(BF16)
    ybn = _rms(yb_ref[...].astype(F32), gyb_ref[...]).astype(BF16)
    xn = x_ref[...] + _dot(yan, woa_ref[...]) + _dot(ybn, wob_ref[...])
    xo_ref[...] = xn
    h2 = _rms(xn, gffn_ref[...]).astype(BF16)
    half = h2.shape[1] // 2
    hi = pltpu.bitcast(h2[:, :half].astype(F32), jnp.uint32)
    lo = pltpu.bitcast(h2[:, half:].astype(F32), jnp.uint32)
    packed = hi | (lo >> 16)
    tm = packed.shape[0]
    for c in range(H2_ROWS):
        h2_ref[pl.ds(c, tm, stride=H2_ROWS), :] = packed[:, c * LANES:(c + 1) * LANES]
    logits = _dot_nt(wrt_ref[...], h2)
    m = jnp.max(logits, axis=0, keepdims=True)
    p = jnp.exp(logits - m)
    afft_ref[...] = p / jnp.sum(p, axis=0, keepdims=True)


def _out_proj(x2, ya2, yb2, lw, tm):
    n, d = x2.shape
    row = lambda w: pl.BlockSpec((tm, w), lambda i: (i, 0))
    return pl.pallas_call(
        _out_proj_kernel,
        grid=(n // tm,),
        in_specs=[row(d), row(SWA_WIDTH), row(MLA_WIDTH),
                  _const_spec((1, SWA_WIDTH)), _const_spec((1, MLA_WIDTH)),
                  _const_spec((SWA_WIDTH, d)), _const_spec((MLA_WIDTH, d)),
                  _const_spec((1, d)), _const_spec((N_EXPERTS, d))],
        out_specs=(row(d), pl.BlockSpec((tm * H2_ROWS, LANES), lambda i: (i, 0)),
                   pl.BlockSpec((N_EXPERTS, tm), lambda i: (0, i))),
        out_shape=(jax.ShapeDtypeStruct((n, d), F32), jax.ShapeDtypeStruct((n * H2_ROWS, LANES), jnp.uint32),
                   jax.ShapeDtypeStruct((N_EXPERTS, n), F32)),
        compiler_params=_cparams("parallel"),
    )(x2, ya2, yb2, lw["g_ya"], lw["g_yb"], lw["w_out_a"], lw["w_out_b"], lw["g_ffn"], lw["w_router_t"])


SEL_CJ = 1024


def _token_cumsum(mask, upper, ones, lstrict):
    mb = mask.astype(BF16)
    within = _dot(mb, upper)
    tot = _dot(mb, ones)
    rowpre = _dot(lstrict, tot.astype(BF16))
    return within, tot, rowpre


def _select_kernel(a_ref, idx_ref, gate_ref, pos_ref, rowpre_ref, *, cap):
    a = a_ref[...]
    g = a.shape[0]
    bits = pltpu.bitcast(a, jnp.int32)

    def count_ge(cand):
        return jnp.sum(jnp.where(bits >= cand, 1.0, 0.0), keepdims=True)

    def radix_step(i, t):
        hi = jnp.left_shift(jnp.int32(1), 29 - 2 * i)
        lo = jnp.left_shift(jnp.int32(1), 28 - 2 * i)
        c3, c2, c1 = t | hi | lo, t | hi, t | lo
        return jnp.where(count_ge(c3) >= cap, c3,
                         jnp.where(count_ge(c2) >= cap, c2, jnp.where(count_ge(c1) >= cap, c1, t)))

    top = jnp.full((1, 1), 1 << 30, jnp.int32)
    thr = jnp.where(count_ge(top) >= cap, top, jnp.zeros((1, 1), jnp.int32))
    thr = lax.fori_loop(0, 15, radix_step, thr)
    ii = lax.broadcasted_iota(jnp.int32, (LANES, LANES), 0)
    jj = lax.broadcasted_iota(jnp.int32, (LANES, LANES), 1)
    upper = jnp.where(ii <= jj, 1.0, 0.0).astype(BF16)
    ones = jnp.ones((LANES, LANES), BF16)
    gi = lax.broadcasted_iota(jnp.int32, (g, g), 0)
    gj = lax.broadcasted_iota(jnp.int32, (g, g), 1)
    lstrict = jnp.where(gj < gi, 1.0, 0.0).astype(BF16)

    gt = bits > thr
    eq = bits == thr
    need = cap - jnp.sum(jnp.where(gt, 1.0, 0.0), keepdims=True)
    eqf = jnp.where(eq, 1.0, 0.0)
    w_eq, _, rp_eq = _token_cumsum(eqf, upper, ones, lstrict)
    tie_rank = w_eq - eqf + rp_eq
    self_ = jnp.where(gt, 1.0, jnp.where(tie_rank < need, eqf, 0.0))
    w, tot, rp = _token_cumsum(self_, upper, ones, lstrict)
    pos_ref[...] = jnp.where(self_ > 0.0, w - 1.0 + rp, -1.0)
    rpt = rp.T
    rowpre_ref[...] = rpt[0:1, :]

    rc = rp + tot
    wt = w.T.astype(BF16)
    pre_hi = jnp.floor(rpt[0:16] * (1.0 / 64.0))
    pre_lo = rpt[0:16] - 64.0 * pre_hi
    pre_hi = pre_hi.astype(BF16)
    pre_lo = pre_lo.astype(BF16)
    at = a.T
    a_h = at.astype(BF16)
    r1 = at - a_h.astype(F32)
    a_m = r1.astype(BF16)
    a_l = (r1 - a_m.astype(F32)).astype(BF16)
    cj_n = min(SEL_CJ, cap)
    for jc in range(cap // cj_n):
        j = (lax.broadcasted_iota(jnp.int32, (1, cj_n), 1) + jc * cj_n).astype(F32)
        rct = jnp.tile(rc, (1, cj_n // LANES))
        row = jnp.sum(jnp.where(rct <= j, 1.0, 0.0), axis=0, keepdims=True)
        grow = lax.broadcasted_iota(jnp.int32, (g, cj_n), 0).astype(F32)
        onehot = jnp.where(grow == row, 1.0, 0.0).astype(BF16)
        wg = _dot(wt, onehot)
        pre = 64.0 * _dot(pre_hi, onehot)[0:1] + _dot(pre_lo, onehot)[0:1]
        jloc = j - pre
        lane = jnp.sum(jnp.where(wg <= jloc, 1.0, 0.0), axis=0, keepdims=True)
        idx_ref[:, jc * cj_n:(jc + 1) * cj_n] = (row * LANES + lane).astype(jnp.int32)
        ag = _dot(a_h, onehot) + _dot(a_m, onehot) + _dot(a_l, onehot)
        lrow = lax.broadcasted_iota(jnp.int32, (LANES, cj_n), 0).astype(F32)
        gate_ref[:, jc * cj_n:(jc + 1) * cj_n] = jnp.sum(jnp.where(lrow == lane, ag, 0.0), axis=0,
                                                          keepdims=True)


def _select(afft, cap):
    e, n = afft.shape
    g = n // LANES
    blk = lambda w: pl.BlockSpec((None, 1, w), lambda i: (i, 0, 0))
    tok = pl.BlockSpec((None, g, LANES), lambda i: (i, 0, 0))
    return pl.pallas_call(
        functools.partial(_select_kernel, cap=cap),
        grid=(e,),
        in_specs=[tok],
        out_specs=(blk(cap), blk(cap), tok, blk(g)),
        out_shape=(jax.ShapeDtypeStruct((e, 1, cap), jnp.int32), jax.ShapeDtypeStruct((e, 1, cap), F32),
                   jax.ShapeDtypeStruct((e, g, LANES), F32), jax.ShapeDtypeStruct((e, 1, g), F32)),
        compiler_params=_cparams("parallel"),
    )(afft.reshape(e, g, LANES))


FFN_TT = 256
FFN_SPLIT = 2
FFN_SLOTS = 3


def _ffn_kernel(idx_ref, h2_hbm, wgu_ref, wd_ref, gate_ref, ye_ref, xbuf, sem):
    tt = xbuf.shape[1] // H2_ROWS
    nj = pl.num_programs(1)
    step = pl.program_id(0) * nj + pl.program_id(1)
    last = pl.num_programs(0) * nj - 1
    nslot = xbuf.shape[0]
    ahead = nslot - 1
    slot = step % nslot

    def row_copy(tok, sl, r):
        return pltpu.make_async_copy(h2_hbm.at[pl.ds(pl.multiple_of(tok * H2_ROWS, H2_ROWS), H2_ROWS), :],
                                     xbuf.at[sl, pl.ds(r * H2_ROWS, H2_ROWS), :], sem.at[sl])

    def issue(s):
        sl = s % nslot
        for r in range(tt):
            row_copy(idx_ref[s * tt + r], sl, r).start()

    def wait_slot(sl):
        pltpu.make_async_copy(h2_hbm.at[pl.ds(0, tt * H2_ROWS), :], xbuf.at[sl], sem.at[sl]).wait()

    @pl.when(step == 0)
    def _():
        for s in range(ahead):
            @pl.when(s <= last)
            def _():
                issue(s)

    @pl.when(step + ahead <= last)
    def _():
        issue(step + ahead)

    wait_slot(slot)
    words = [xbuf[slot, pl.ds(c, tt, stride=H2_ROWS), :] for c in range(H2_ROWS)]
    x = jnp.concatenate([pltpu.bitcast(w & jnp.uint32(0xFFFF0000), F32).astype(BF16) for w in words]
                        + [pltpu.bitcast(w << 16, F32).astype(BF16) for w in words], axis=1)
    for part in range(FFN_SPLIT):
        r = slice(part * (tt // FFN_SPLIT), (part + 1) * (tt // FFN_SPLIT))
        au = _dot(x[r], wgu_ref[...])
        a = au[:, :D_EXPERT]
        u = au[:, D_EXPERT:]
        act = (a / (1.0 + jnp.exp(-a)) * u).astype(BF16)
        ye_ref[r, :] = (_dot(act, wd_ref[...]) * gate_ref[r, :]).astype(BF16)


def _ffn(idx_flat, h2p, wgu, wd, gate, cap):
    e, d = wgu.shape[0], wgu.shape[1]
    tt = _pick(cap, FFN_TT)
    return pl.pallas_call(
        _ffn_kernel,
        grid_spec=pltpu.PrefetchScalarGridSpec(
            num_scalar_prefetch=1,
            grid=(e, cap // tt),
            in_specs=[pl.BlockSpec(memory_space=pl.ANY),
                      pl.BlockSpec((None, d, 2 * D_EXPERT), lambda i, j, idx: (i, 0, 0)),
                      pl.BlockSpec((None, D_EXPERT, d), lambda i, j, idx: (i, 0, 0)),
                      pl.BlockSpec((None, tt, 1), lambda i, j, idx: (i, j, 0))],
            out_specs=pl.BlockSpec((None, tt, d), lambda i, j, idx: (i, j, 0)),
            scratch_shapes=[pltpu.VMEM((FFN_SLOTS, tt * H2_ROWS, LANES), jnp.uint32),
                            pltpu.SemaphoreType.DMA((FFN_SLOTS,))]),
        out_shape=jax.ShapeDtypeStruct((e, cap, d), BF16),
        compiler_params=_cparams("arbitrary", "arbitrary"),
    )(idx_flat, h2p, wgu, wd, gate)


CAST_ROWS = 1024
BF16_SUBLANES = 16


def _row_block(n, limit):
    return max(r for r in range(BF16_SUBLANES, limit + 1, BF16_SUBLANES) if n % r == 0)


def _cast_gu_kernel(g_ref, u_ref, o_ref):
    o_ref[:, :D_EXPERT] = g_ref[...].astype(BF16)
    o_ref[:, D_EXPERT:] = u_ref[...].astype(BF16)


def _cast_kernel(w_ref, o_ref):
    o_ref[...] = w_ref[...].astype(BF16)


def _cast_experts(w_gate, w_up, w_down):
    l, e, d, f = w_gate.shape
    blk = lambda r, c: pl.BlockSpec((None, None, r, c), lambda a, b, i: (a, b, i, 0))
    rg = _row_block(d, CAST_ROWS // 2)
    wgu = pl.pallas_call(
        _cast_gu_kernel, grid=(l, e, d // rg),
        in_specs=[blk(rg, f), blk(rg, f)], out_specs=blk(rg, 2 * f),
        out_shape=jax.ShapeDtypeStruct((l, e, d, 2 * f), BF16),
        compiler_params=_cparams("parallel", "parallel", "parallel"),
    )(w_gate, w_up)
    rd = _row_block(f, CAST_ROWS)
    wd = pl.pallas_call(
        _cast_kernel, grid=(l, e, f // rd),
        in_specs=[blk(rd, d)], out_specs=blk(rd, d),
        out_shape=jax.ShapeDtypeStruct((l, e, f, d), BF16),
        compiler_params=_cparams("parallel", "parallel", "parallel"),
    )(w_down)
    return wgu, wd


CMB_TT = 128
CMB_BLK = 16
CMB_KC = 256
CMB_ROWS = N_EXPERTS * CMB_TT + N_EXPERTS * 2 * CMB_BLK + CMB_KC


def _combine_kernel(rowpre_ref, x_ref, pos_ref, gfin_ref, ye_hbm, o_ref, buf, sem, acc_ref, *, g, final):
    i = pl.program_id(0)
    last = pl.num_programs(0) - 1
    slot = i % 2

    def plan(tile):
        starts, nblks, offs = [], [], []
        off = jnp.int32(0)
        for e in range(N_EXPERTS):
            lo = rowpre_ref[e * (g + 1) + tile]
            hi = rowpre_ref[e * (g + 1) + tile + 1]
            start = (lo >> 4) << 4
            nblk = jnp.where(hi > lo, (hi - start + (CMB_BLK - 1)) >> 4, 0)
            starts.append(start)
            nblks.append(nblk)
            offs.append(off)
            off = off + nblk
        return starts, nblks, offs, off

    def block_copy(e, src_row, dst_row, sl):
        return pltpu.make_async_copy(ye_hbm.at[e, pl.ds(src_row, CMB_BLK), :],
                                     buf.at[sl, pl.ds(dst_row, CMB_BLK), :], sem.at[sl])

    def issue(tile, sl):
        starts, nblks, offs, _ = plan(tile)
        for e in range(N_EXPERTS):
            def body(b, c, e=e):
                block_copy(e, pl.multiple_of(starts[e] + b * CMB_BLK, CMB_BLK),
                           pl.multiple_of((offs[e] + b) * CMB_BLK, CMB_BLK), sl).start()
                return c
            lax.fori_loop(0, nblks[e], body, 0)

    def wait(tile, sl):
        rows = plan(tile)[3] * CMB_BLK

        @pl.when(rows > 0)
        def _():
            pltpu.make_async_copy(ye_hbm.at[0, pl.ds(0, rows), :], buf.at[sl, pl.ds(0, rows), :],
                                  sem.at[sl]).wait()

    @pl.when(i == 0)
    def _():
        buf[...] = jnp.zeros_like(buf)
        issue(0, 0)

    @pl.when(i < last)
    def _():
        issue(i + 1, 1 - slot)

    wait(i, slot)
    starts, nblks, offs, total = plan(i)
    pos = jnp.concatenate([pos_ref[...], jnp.zeros((LANES - N_EXPERTS, CMB_TT), F32)], axis=0).T
    cols = []
    for e in range(N_EXPERTS):
        pe = pos[:, e:e + 1]
        shift = (offs[e] * CMB_BLK - starts[e]).astype(F32)
        col = jnp.where(pe >= 0.0, pe + shift, -1.0)
        cols.append(jnp.broadcast_to(col, (CMB_TT, CMB_KC)))
    acc_ref[...] = x_ref[...]

    def chunk(k, c):
        k0 = pl.multiple_of(k * CMB_KC, CMB_KC)
        r = (lax.broadcasted_iota(jnp.int32, (CMB_TT, CMB_KC), 1) + k0).astype(F32)
        hit = jnp.where(cols[0] == r, 1.0, 0.0)
        for e in range(1, N_EXPERTS):
            hit = hit + jnp.where(cols[e] == r, 1.0, 0.0)
        acc_ref[...] += _dot(hit.astype(BF16), buf[slot, pl.ds(k0, CMB_KC), :])
        return c

    lax.fori_loop(0, (total * CMB_BLK + (CMB_KC - 1)) // CMB_KC, chunk, 0)
    o_ref[...] = _rms(acc_ref[...], gfin_ref[...]) if final else acc_ref[...]


def _combine(rowpre_flat, x2, pos, g_final, ye, g, final):
    n, d = x2.shape
    assert n == g * CMB_TT
    return pl.pallas_call(
        functools.partial(_combine_kernel, g=g, final=final),
        grid_spec=pltpu.PrefetchScalarGridSpec(
            num_scalar_prefetch=1,
            grid=(g,),
            in_specs=[pl.BlockSpec((CMB_TT, d), lambda i, rp: (i, 0)),
                      pl.BlockSpec((N_EXPERTS, CMB_TT), lambda i, rp: (0, i)),
                      pl.BlockSpec((1, d), lambda i, rp: (0, 0)),
                      pl.BlockSpec(memory_space=pl.ANY)],
            out_specs=pl.BlockSpec((CMB_TT, d), lambda i, rp: (i, 0)),
            scratch_shapes=[pltpu.VMEM((2, CMB_ROWS, d), BF16), pltpu.SemaphoreType.DMA((2,)),
                            pltpu.VMEM((CMB_TT, d), F32)]),
        out_shape=jax.ShapeDtypeStruct((n, d), F32),
        compiler_params=_cparams("arbitrary"),
    )(rowpre_flat, x2, pos, g_final, ye)


def _rot_half_cols(w):
    half = w.shape[-1] // 2
    return jnp.concatenate([-w[..., half:], w[..., :half]], axis=-1)


def _prep_layer(l, g_attn, w_in, sink, g_cq, w_uq, g_ckv, w_uk, w_uv, g_ya, g_yb, w_out, g_ffn,
                w_router):
    wi = w_in[l]
    w_kr = wi[:, SRC_KR:SRC_KR + QK_ROPE]
    zpad = jnp.zeros((D_MODEL, 128 - QK_ROPE), F32)
    w_in_ext = jnp.concatenate([wi[:, SRC_KA:SRC_VA], wi[:, SRC_CQ:SRC_KR], w_kr, zpad,
                                _rot_half_cols(w_kr), zpad], axis=1)
    w_in_t = jnp.concatenate([wi[:, SRC_QA:SRC_KA], wi[:, SRC_VA:SRC_CQ]], axis=1).T
    wq = w_uq[l].reshape(Q_LORA, MLA_HEADS, QK_NOPE + QK_ROPE)
    w_uq_main = w_uq[l]
    w_uq_rot = _rot_half_cols(wq[:, :, QK_NOPE:]).reshape(Q_LORA, MLA_HEADS * QK_ROPE)
    return dict(
        g_attn=g_attn[l][None], w_in=w_in_ext.astype(BF16), w_in_t=w_in_t.astype(BF16), sink=sink[l],
        g_cq=g_cq[l][None], w_uq_main_t=w_uq_main.T.astype(BF16), w_uq_rot_t=w_uq_rot.T.astype(BF16),
        g_ckv=g_ckv[l][None], w_uk=w_uk[l].astype(BF16), w_uvt=w_uv[l].T.astype(BF16),
        g_ya=g_ya[l][None], g_yb=g_yb[l][None],
        w_out_a=w_out[l][:SWA_WIDTH].astype(BF16), w_out_b=w_out[l][SWA_WIDTH:].astype(BF16),
        g_ffn=g_ffn[l][None], w_router_t=w_router[l].T.astype(BF16), layer=l,
    )


def _rope_tables(s):
    inv = 1.0 / (ROPE_THETA ** (jnp.arange(0, QK_ROPE, 2, dtype=F32) / QK_ROPE))
    ang = jnp.arange(s, dtype=F32)[:, None] * inv[None, :]
    z = jnp.zeros((s, 128 - QK_ROPE), F32)
    cos, sin = jnp.cos(ang), jnp.sin(ang)
    cos128 = jnp.concatenate([cos, cos, z], axis=1)
    sin128 = jnp.concatenate([sin, sin, z], axis=1)
    return cos128, sin128, cos128.T, sin128.T


def _pick(n, pref):
    t = pref
    while n % t:
        t //= 2
    return t


def _layer(x, lw, experts, raw, tables, g_final, final):
    b, s, d = x.shape
    n = b * s
    qat, ka, vat, qmt, km, vt = _in_proj(x, lw, tables, _pick(s, MLA_TK))
    ya = _swa(qat, ka, vat, lw["sink"])
    if experts is None:
        yb, *experts = _mla(qmt, km, vt, raw, lw["layer"])
    else:
        yb = _mla(qmt, km, vt)
    x2, h2, afft = _out_proj(x.reshape(n, d), ya.reshape(n, SWA_WIDTH), yb.reshape(n, MLA_WIDTH), lw,
                             _pick(n, 512))
    cap = CAPACITY_FACTOR * n // N_EXPERTS
    g = n // LANES
    idx, gate, pos, rowpre = _select(afft, cap)
    ye = _ffn(idx.reshape(-1), h2, *experts, gate.reshape(N_EXPERTS, cap, 1), cap)
    bounds = jnp.concatenate([rowpre.reshape(N_EXPERTS, g), jnp.full((N_EXPERTS, 1), cap, F32)], axis=1)
    x2 = _combine(bounds.astype(jnp.int32).reshape(-1), x2, pos.reshape(N_EXPERTS, n), g_final, ye, g, final)
    return x2.reshape(b, s, d), experts


def _trunk(x, layers, experts, raw, g_final):
    b, s, d = x.shape
    tables = _rope_tables(s)
    produced = []
    for l, lw in enumerate(layers):
        x, ex = _layer(x, lw, None if experts is None else experts[l], raw, tables, g_final[None],
                       final=lw is layers[-1])
        produced.append(ex)
    return x, produced


def kernel(x_prompt, x_sample, g_attn, w_in, sink, g_cq, w_uq, g_ckv, w_uk, w_uv, g_ya, g_yb, w_out,
           g_ffn, w_router, w_gate, w_up, w_down, g_final):
    depth = w_in.shape[0]
    layers = [_prep_layer(l, g_attn, w_in, sink, g_cq, w_uq, g_ckv, w_uk, w_uv, g_ya, g_yb, w_out,
                          g_ffn, w_router) for l in range(depth)]
    raw = (w_gate, w_up, w_down)
    b, s, _ = x_prompt.shape
    steps = b * MLA_HEADS * (s // (MLA_NSUB * MLA_TQ)) if s % (MLA_NSUB * MLA_TQ) == 0 else 0
    experts = None
    if not steps or _cast_plan(steps, *w_gate.shape[1:]) is None:
        wgu, wd = _cast_experts(*raw)
        experts = [(wgu[l], wd[l]) for l in range(depth)]
    y_prompt, experts = _trunk(x_prompt, layers, experts, raw, g_final)
    y_sample, _ = _trunk(x_sample, layers, experts, raw, g_final)
    return y_prompt, y_sample
```

```python
import functools

import numpy as np
import jax
import jax.numpy as jnp
from jax import lax
from jax.experimental import pallas as pl
from jax.experimental.pallas import tpu as pltpu

F32 = jnp.float32
BF16 = jnp.bfloat16

D_MODEL = 2048
HEAD_DIM = 128
SWA_HEADS = 8
SWA_KV_HEADS = 2
SWA_GROUP = SWA_HEADS // SWA_KV_HEADS
WINDOW = 128
BLOCK = 128
MLA_HEADS = 8
QK_NOPE = 128
QK_ROPE = 64
V_DIM = 128
Q_LORA = 512
KV_LORA = 512
ROPE_THETA = 10000.0
SWA_WIDTH = SWA_HEADS * HEAD_DIM
SWA_KV_WIDTH = SWA_KV_HEADS * HEAD_DIM
MLA_WIDTH = MLA_HEADS * V_DIM
N_EXPERTS = 16
D_EXPERT = 1408
CAPACITY_FACTOR = 2
EPS = 1e-6
LANES = 128

SRC_QA = 0
SRC_KA = SRC_QA + SWA_WIDTH
SRC_VA = SRC_KA + SWA_KV_WIDTH
SRC_CQ = SRC_VA + SWA_KV_WIDTH
SRC_KR = SRC_CQ + Q_LORA + KV_LORA
OFF_KA = 0
OFF_CQ = OFF_KA + SWA_KV_WIDTH
OFF_CKV = OFF_CQ + Q_LORA
OFF_KR = OFF_CKV + KV_LORA
OFF_KRR = OFF_KR + 128
IN_EXT = OFF_KRR + 128
TOFF_QA = 0
TOFF_VA = TOFF_QA + SWA_WIDTH
IN_T = TOFF_VA + SWA_KV_WIDTH
MLA_QK = 256

SWA_SCALE = HEAD_DIM ** -0.5
MLA_SCALE = (QK_NOPE + QK_ROPE) ** -0.5 * float(np.log2(np.e))
NEG_BIG = -1e30
MLA_TQ = 512
MLA_TK = 512

V7X_VMEM_BYTES = 64 * 1024 * 1024
VMEM_LIMIT = V7X_VMEM_BYTES - 8 * 1024 * 1024


def _cparams(*sem):
    return pltpu.CompilerParams(dimension_semantics=sem, vmem_limit_bytes=VMEM_LIMIT)


def _const_spec(shape):
    nd = len(shape)
    return pl.BlockSpec(shape, lambda *_: (0,) * nd, pipeline_mode=pl.Buffered(1))


def _rms(x, g):
    return x * lax.rsqrt(jnp.mean(x * x, axis=-1, keepdims=True) + EPS) * g


def _dot(a, b):
    return jnp.dot(a, b, preferred_element_type=F32)


def _dot_nt(a, b):
    return lax.dot_general(a, b, (((1,), (1,)), ((), ())), preferred_element_type=F32)


def _in_proj_kernel(x_ref, g_ref, win_ref, wint_ref, gcq_ref, wqmt_ref, wqrt_ref, gckv_ref, wuk_ref,
                    wuvt_ref, cos_ref, sin_ref, cost_ref, sint_ref,
                    qat_ref, ka_ref, vat_ref, qmt_ref, km_ref, vt_ref):
    h = _rms(x_ref[...], g_ref[...]).astype(BF16)
    proj = _dot(h, win_ref[...])
    proj_t = _dot_nt(wint_ref[...], h)
    qat_ref[...] = (proj_t[TOFF_QA:TOFF_VA] * SWA_SCALE).astype(BF16)
    vat_ref[...] = proj_t[TOFF_VA:IN_T].astype(BF16)
    ka_ref[...] = proj[:, OFF_KA:OFF_CQ].astype(BF16)
    cos = cos_ref[...]
    sin = sin_ref[...]
    krope = (proj[:, OFF_KR:OFF_KRR] * cos + proj[:, OFF_KRR:IN_EXT] * sin).astype(BF16)
    cqn = _rms(proj[:, OFF_CQ:OFF_CKV], gcq_ref[...]).astype(BF16)
    qmain = _dot_nt(wqmt_ref[...], cqn)
    qrot = _dot_nt(wqrt_ref[...], cqn)
    cos_t = cost_ref[0:QK_ROPE, :]
    sin_t = sint_ref[0:QK_ROPE, :]
    qk = QK_NOPE + QK_ROPE
    zero_rows = jnp.zeros((MLA_QK - qk, qmain.shape[1]), BF16)
    for hh in range(MLA_HEADS):
        lo = hh * MLA_QK
        qmt_ref[lo:lo + QK_NOPE, :] = (qmain[hh * qk:hh * qk + QK_NOPE] * MLA_SCALE).astype(BF16)
        rp = (qmain[hh * qk + QK_NOPE:(hh + 1) * qk] * cos_t
              + qrot[hh * QK_ROPE:(hh + 1) * QK_ROPE] * sin_t)
        qmt_ref[lo + QK_NOPE:lo + qk, :] = (rp * MLA_SCALE).astype(BF16)
        qmt_ref[lo + qk:lo + MLA_QK, :] = zero_rows
    ckvn = _rms(proj[:, OFF_CKV:OFF_KR], gckv_ref[...]).astype(BF16)
    kn = _dot(ckvn, wuk_ref[...])
    for hh in range(MLA_HEADS):
        lo = hh * MLA_QK
        km_ref[:, lo:lo + QK_NOPE] = kn[:, hh * QK_NOPE:(hh + 1) * QK_NOPE].astype(BF16)
        km_ref[:, lo + QK_NOPE:lo + MLA_QK] = krope
    vt_ref[...] = _dot_nt(wuvt_ref[...], ckvn).astype(BF16)


def _in_proj(x, lw, tables, tm):
    b, s, d = x.shape
    row = lambda w: pl.BlockSpec((None, tm, w), lambda i, j: (i, j, 0))
    tab = pl.BlockSpec((tm, 128), lambda i, j: (j, 0))
    tab_t = pl.BlockSpec((128, tm), lambda i, j: (0, j))
    col = lambda w: pl.BlockSpec((None, w, tm), lambda i, j: (i, 0, j))
    out_shapes = (
        jax.ShapeDtypeStruct((b, SWA_WIDTH, s), BF16),
        jax.ShapeDtypeStruct((b, s, SWA_KV_WIDTH), BF16),
        jax.ShapeDtypeStruct((b, SWA_KV_WIDTH, s), BF16),
        jax.ShapeDtypeStruct((b, MLA_HEADS * MLA_QK, s), BF16),
        jax.ShapeDtypeStruct((b, s, MLA_HEADS * MLA_QK), BF16),
        jax.ShapeDtypeStruct((b, s // tm, MLA_WIDTH, tm), BF16),
    )
    return pl.pallas_call(
        _in_proj_kernel,
        grid=(b, s // tm),
        in_specs=[
            row(d),
            _const_spec((1, d)),
            _const_spec((d, IN_EXT)),
            _const_spec((IN_T, d)),
            _const_spec((1, Q_LORA)),
            _const_spec((MLA_HEADS * (QK_NOPE + QK_ROPE), Q_LORA)),
            _const_spec((MLA_HEADS * QK_ROPE, Q_LORA)),
            _const_spec((1, KV_LORA)),
            _const_spec((KV_LORA, MLA_HEADS * QK_NOPE)),
            _const_spec((MLA_WIDTH, KV_LORA)),
            tab, tab, tab_t, tab_t,
        ],
        out_specs=(col(SWA_WIDTH), row(SWA_KV_WIDTH), col(SWA_KV_WIDTH), col(MLA_HEADS * MLA_QK),
                   row(MLA_HEADS * MLA_QK),
                   pl.BlockSpec((None, None, MLA_WIDTH, tm), lambda i, j: (i, j, 0, 0))),
        out_shape=out_shapes,
        compiler_params=_cparams("parallel", "parallel"),
    )(x, lw["g_attn"], lw["w_in"], lw["w_in_t"], lw["g_cq"], lw["w_uq_main_t"], lw["w_uq_rot_t"], lw["g_ckv"],
      lw["w_uk"], lw["w_uvt"], *tables)


def _alibi_slopes():
    return [float(np.float32(2.0 ** (-8.0 * (i + 1) / SWA_HEADS))) for i in range(SWA_HEADS)]


SWA_TQ = 1024


def _swa_kernel(sink_ref, qt_ref, kp_ref, kc_ref, kn_ref, vtp_ref, vtc_ref, vtn_ref, o_ref, *, seq):
    tq = qt_ref.shape[1]
    base = pl.program_id(1) * tq
    ki = lax.broadcasted_iota(jnp.int32, (3 * BLOCK, BLOCK), 0)
    qj = lax.broadcasted_iota(jnp.int32, (3 * BLOCK, BLOCK), 1)
    rel = ki - BLOCK - qj
    in_band = jnp.abs(rel) <= WINDOW
    dist = jnp.abs(rel).astype(F32)
    slopes = _alibi_slopes()
    kall = jnp.concatenate([kp_ref[...], kc_ref[...], kn_ref[...]], axis=0)
    vtall = jnp.concatenate([vtp_ref[...], vtc_ref[...], vtn_ref[...]], axis=1)
    for sb in range(tq // BLOCK):
        spos = base + sb * BLOCK + ki - BLOCK
        valid = in_band & (spos >= 0) & (spos < seq)
        cols = slice(sb * BLOCK, (sb + 1) * BLOCK)
        win = slice(sb * BLOCK, (sb + 3) * BLOCK)
        for kvh in range(SWA_KV_HEADS):
            kh = kall[win, kvh * HEAD_DIM:(kvh + 1) * HEAD_DIM]
            vth = vtall[kvh * HEAD_DIM:(kvh + 1) * HEAD_DIM, win]
            heads = [kvh * SWA_GROUP + g for g in range(SWA_GROUP)]
            qt = jnp.concatenate([qt_ref[hd * HEAD_DIM:(hd + 1) * HEAD_DIM, cols] for hd in heads], axis=1)
            raw = _dot(kh, qt)
            logits = jnp.concatenate(
                [jnp.where(valid, raw[:, g * BLOCK:(g + 1) * BLOCK] - slopes[hd] * dist, NEG_BIG)
                 for g, hd in enumerate(heads)], axis=1)
            sk = jnp.concatenate([jnp.full((1, BLOCK), sink_ref[hd], F32) for hd in heads], axis=1)
            m = jnp.maximum(jnp.max(logits, axis=0, keepdims=True), sk)
            p = jnp.exp(logits - m)
            denom = jnp.sum(p, axis=0, keepdims=True) + jnp.exp(sk - m)
            ot = _dot(vth, p.astype(BF16)) / denom
            for g, hd in enumerate(heads):
                o_ref[cols, hd * HEAD_DIM:(hd + 1) * HEAD_DIM] = ot[:, g * BLOCK:(g + 1) * BLOCK].T.astype(BF16)


def _swa(qat, ka, vat, sink):
    b, s, _ = ka.shape
    tq = _pick(s, SWA_TQ)
    r = tq // BLOCK
    nb = s // BLOCK
    prev = lambda j: jnp.maximum(j * r - 1, 0)
    nxt = lambda j: jnp.minimum(j * r + r, nb - 1)
    kedge = lambda f: pl.BlockSpec((None, BLOCK, SWA_KV_WIDTH), lambda i, j: (i, f(j), 0))
    vedge = lambda f: pl.BlockSpec((None, SWA_KV_WIDTH, BLOCK), lambda i, j: (i, 0, f(j)))
    return pl.pallas_call(
        functools.partial(_swa_kernel, seq=s),
        grid=(b, s // tq),
        in_specs=[pl.BlockSpec(memory_space=pltpu.SMEM),
                  pl.BlockSpec((None, SWA_WIDTH, tq), lambda i, j: (i, 0, j)),
                  kedge(prev), pl.BlockSpec((None, tq, SWA_KV_WIDTH), lambda i, j: (i, j, 0)), kedge(nxt),
                  vedge(prev), pl.BlockSpec((None, SWA_KV_WIDTH, tq), lambda i, j: (i, 0, j)), vedge(nxt)],
        out_specs=pl.BlockSpec((None, tq, SWA_WIDTH), lambda i, j: (i, j, 0)),
        out_shape=jax.ShapeDtypeStruct((b, s, SWA_WIDTH), BF16),
        compiler_params=_cparams("parallel", "parallel"),
    )(sink, qat, ka, ka, ka, vat, vat, vat)


MLA_NSUB = 2
MLA_SUM_ROWS = 16


def _mla_kernel(q_ref, k_ref, vt_ref, *rest, cast):
    if cast:
        g_ref, u_ref, d_ref, o_ref, wgu_ref, wd_ref, st_ref = rest
        wgu_ref[:, :D_EXPERT] = g_ref[...].astype(BF16)
        wgu_ref[:, D_EXPERT:] = u_ref[...].astype(BF16)
        wd_ref[...] = d_ref[...].astype(BF16)
    else:
        o_ref, st_ref = rest
    nsub, _, tk, tq = st_ref.shape
    nchunk = vt_ref.shape[0]
    assert nchunk % 2 == 0
    qs = [q_ref[:, t * tq:(t + 1) * tq] for t in range(nsub)]

    def produce(t, slot, c):
        k0 = pl.multiple_of(c * tk, tk)
        st = _dot(k_ref[pl.ds(k0, tk), :], qs[t])
        st_ref[t, slot] = st
        return jnp.max(st, axis=0, keepdims=True)

    ones_rows = jnp.ones((MLA_SUM_ROWS, tk), BF16)

    def consume(t, slot, c, mx, state):
        m, acc = state
        m_new = jnp.maximum(m, mx)
        alpha = jnp.exp2(m - m_new)
        p = jnp.exp2(st_ref[t, slot] - m_new)
        vext = jnp.concatenate([vt_ref[c], ones_rows], axis=0)
        acc = alpha * acc + _dot(vext, p.astype(BF16))
        return m_new, acc

    def pair(c, mx0, states, produce_next):
        mx1 = [produce(t, 1, c + 1) for t in range(nsub)]
        states = [consume(t, 0, c, mx0[t], states[t]) for t in range(nsub)]
        if produce_next:
            mx0 = [produce(t, 0, c + 2) for t in range(nsub)]
        states = [consume(t, 1, c + 1, mx1[t], states[t]) for t in range(nsub)]
        return mx0, states

    def body(i, carry):
        return pair(2 * i, *carry, produce_next=True)

    state0 = (jnp.full((1, tq), NEG_BIG, F32), jnp.zeros((V_DIM + MLA_SUM_ROWS, tq), F32))
    init = ([produce(t, 0, 0) for t in range(nsub)], [state0] * nsub)
    mx0, states = lax.fori_loop(0, nchunk // 2 - 1, body, init, unroll=2)
    _, states = pair(nchunk - 2, mx0, states, produce_next=False)
    for t in range(nsub):
        acc = states[t][1]
        o_ref[t * tq:(t + 1) * tq, :] = (acc[:V_DIM] / acc[V_DIM:V_DIM + 1]).T.astype(BF16)


def _cast_plan(steps, e, d, f):
    if (e * d) % steps:
        return None
    rows_gu = e * d // steps
    rows_d = [r for r in range(BF16_SUBLANES, f + 1, BF16_SUBLANES) if f % r == 0 and e * (f // r) <= steps]
    if rows_gu % BF16_SUBLANES or d % rows_gu or not rows_d:
        return None
    return rows_gu, rows_d[0]


def _mla(qmt, km, vt, raw=None, layer=0):
    b, s, _ = km.shape
    nchunk, tk = vt.shape[1], vt.shape[3]
    tq = _pick(s, MLA_TQ)
    nsub = MLA_NSUB if s % (MLA_NSUB * tq) == 0 else 1
    nq = s // (nsub * tq)
    in_specs = [pl.BlockSpec((None, MLA_QK, nsub * tq), lambda bb, h, i: (bb, h, i)),
                pl.BlockSpec((None, s, MLA_QK), lambda bb, h, i: (bb, 0, h)),
                pl.BlockSpec((None, nchunk, V_DIM, tk), lambda bb, h, i: (bb, 0, h, 0))]
    out_specs = [pl.BlockSpec((None, nsub * tq, V_DIM), lambda bb, h, i: (bb, i, h))]
    out_shape = [jax.ShapeDtypeStruct((b, s, MLA_WIDTH), BF16)]
    args = [qmt, km, vt]
    if raw is not None:
        w_gate, w_up, w_down = raw
        _, e, d, f = w_gate.shape
        rows_gu, rows_d = _cast_plan(b * MLA_HEADS * nq, e, d, f)
        step = lambda bb, h, i: (bb * MLA_HEADS + h) * nq + i
        per_e = d // rows_gu
        gu_idx = lambda bb, h, i: (step(bb, h, i) // per_e, step(bb, h, i) % per_e, 0)
        nd = f // rows_d
        d_step = lambda bb, h, i: jnp.minimum(step(bb, h, i), e * nd - 1)
        d_idx = lambda bb, h, i: (d_step(bb, h, i) // nd, d_step(bb, h, i) % nd, 0)
        in_specs += [pl.BlockSpec((None, None, rows_gu, f), lambda *a: (layer,) + gu_idx(*a)),
                     pl.BlockSpec((None, None, rows_gu, f), lambda *a: (layer,) + gu_idx(*a)),
                     pl.BlockSpec((None, None, rows_d, d), lambda *a: (layer,) + d_idx(*a))]
        out_specs += [pl.BlockSpec((None, rows_gu, 2 * f), gu_idx), pl.BlockSpec((None, rows_d, d), d_idx)]
        out_shape += [jax.ShapeDtypeStruct((e, d, 2 * f), BF16), jax.ShapeDtypeStruct((e, f, d), BF16)]
        args += [w_gate, w_up, w_down]
    outs = pl.pallas_call(
        functools.partial(_mla_kernel, cast=raw is not None),
        grid=(b, MLA_HEADS, nq),
        in_specs=in_specs,
        out_specs=out_specs,
        out_shape=out_shape,
        scratch_shapes=[pltpu.VMEM((nsub, 2, tk, tq), F32)],
        compiler_params=_cparams("arbitrary", "arbitrary", "arbitrary"),
    )(*args)
    return outs[0] if raw is None else outs


H2_ROWS = D_MODEL // 2 // LANES


def _out_proj_kernel(x_ref, ya_ref, yb_ref, gya_ref, gyb_ref, woa_ref, wob_ref, gffn_ref, wrt_ref,
                     xo_ref, h2_ref, afft_ref):
    yan = _rms(ya_ref[...].astype(F32), gya_ref[...]).astype(BF16)
    ybn = _rms(yb_ref[...].astype(F32), gyb_ref[...]).astype(BF16)
    xn = x_ref[...] + _dot(yan, woa_ref[...]) + _dot(ybn, wob_ref[...])
    xo_ref[...] = xn
    h2 = _rms(xn, gffn_ref[...]).astype(BF16)
    half = h2.shape[1] // 2
    hi = pltpu.bitcast(h2[:, :half].astype(F32), jnp.uint32)
    lo = pltpu.bitcast(h2[:, half:].astype(F32), jnp.uint32)
    packed = hi | (lo >> 16)
    tm = packed.shape[0]
    for c in range(H2_ROWS):
        h2_ref[pl.ds(c, tm, stride=H2_ROWS), :] = packed[:, c * LANES:(c + 1) * LANES]
    logits = _dot_nt(wrt_ref[...], h2)
    m = jnp.max(logits, axis=0, keepdims=True)
    p = jnp.exp(logits - m)
    afft_ref[...] = p / jnp.sum(p, axis=0, keepdims=True)


def _out_proj(x2, ya2, yb2, lw, tm):
    n, d = x2.shape
    row = lambda w: pl.BlockSpec((tm, w), lambda i: (i, 0))
    return pl.pallas_call(
        _out_proj_kernel,
        grid=(n // tm,),
        in_specs=[row(d), row(SWA_WIDTH), row(MLA_WIDTH),
                  _const_spec((1, SWA_WIDTH)), _const_spec((1, MLA_WIDTH)),
                  _const_spec((SWA_WIDTH, d)), _const_spec((MLA_WIDTH, d)),
                  _const_spec((1, d)), _const_spec((N_EXPERTS, d))],
        out_specs=(row(d), pl.BlockSpec((tm * H2_ROWS, LANES), lambda i: (i, 0)),
                   pl.BlockSpec((N_EXPERTS, tm), lambda i: (0, i))),
        out_shape=(jax.ShapeDtypeStruct((n, d), F32), jax.ShapeDtypeStruct((n * H2_ROWS, LANES), jnp.uint32),
                   jax.ShapeDtypeStruct((N_EXPERTS, n), F32)),
        compiler_params=_cparams("parallel"),
    )(x2, ya2, yb2, lw["g_ya"], lw["g_yb"], lw["w_out_a"], lw["w_out_b"], lw["g_ffn"], lw["w_router_t"])


SEL_CJ = 1024


def _token_cumsum(mask, upper, ones, lstrict):
    mb = mask.astype(BF16)
    within = _dot(mb, upper)
    tot = _dot(mb, ones)
    rowpre = _dot(lstrict, tot.astype(BF16))
    return within, tot, rowpre


def _select_kernel(a_ref, idx_ref, gate_ref, pos_ref, rowpre_ref, *, cap):
    a = a_ref[...]
    g = a.shape[0]
    bits = pltpu.bitcast(a, jnp.int32)

    def count_ge(cand):
        return jnp.sum(jnp.where(bits >= cand, 1.0, 0.0), keepdims=True)

    def radix_step(i, t):
        hi = jnp.left_shift(jnp.int32(1), 29 - 2 * i)
        lo = jnp.left_shift(jnp.int32(1), 28 - 2 * i)
        c3, c2, c1 = t | hi | lo, t | hi, t | lo
        return jnp.where(count_ge(c3) >= cap, c3,
                         jnp.where(count_ge(c2) >= cap, c2, jnp.where(count_ge(c1) >= cap, c1, t)))

    top = jnp.full((1, 1), 1 << 30, jnp.int32)
    thr = jnp.where(count_ge(top) >= cap, top, jnp.zeros((1, 1), jnp.int32))
    thr = lax.fori_loop(0, 15, radix_step, thr)
    ii = lax.broadcasted_iota(jnp.int32, (LANES, LANES), 0)
    jj = lax.broadcasted_iota(jnp.int32, (LANES, LANES), 1)
    upper = jnp.where(ii <= jj, 1.0, 0.0).astype(BF16)
    ones = jnp.ones((LANES, LANES), BF16)
    gi = lax.broadcasted_iota(jnp.int32, (g, g), 0)
    gj = lax.broadcasted_iota(jnp.int32, (g, g), 1)
    lstrict = jnp.where(gj < gi, 1.0, 0.0).astype(BF16)

    gt = bits > thr
    eq = bits == thr
    need = cap - jnp.sum(jnp.where(gt, 1.0, 0.0), keepdims=True)
    eqf = jnp.where(eq, 1.0, 0.0)
    w_eq, _, rp_eq = _token_cumsum(eqf, upper, ones, lstrict)
    tie_rank = w_eq - eqf + rp_eq
    self_ = jnp.where(gt, 1.0, jnp.where(tie_rank < need, eqf, 0.0))
    w, tot, rp = _token_cumsum(self_, upper, ones, lstrict)
    pos_ref[...] = jnp.where(self_ > 0.0, w - 1.0 + rp, -1.0)
    rpt = rp.T
    rowpre_ref[...] = rpt[0:1, :]

    rc = rp + tot
    wt = w.T.astype(BF16)
    pre_hi = jnp.floor(rpt[0:16] * (1.0 / 64.0))
    pre_lo = rpt[0:16] - 64.0 * pre_hi
    pre_hi = pre_hi.astype(BF16)
    pre_lo = pre_lo.astype(BF16)
    at = a.T
    a_h = at.astype(BF16)
    r1 = at - a_h.astype(F32)
    a_m = r1.astype(BF16)
    a_l = (r1 - a_m.astype(F32)).astype(BF16)
    cj_n = min(SEL_CJ, cap)
    for jc in range(cap // cj_n):
        j = (lax.broadcasted_iota(jnp.int32, (1, cj_n), 1) + jc * cj_n).astype(F32)
        rct = jnp.tile(rc, (1, cj_n // LANES))
        row = jnp.sum(jnp.where(rct <= j, 1.0, 0.0), axis=0, keepdims=True)
        grow = lax.broadcasted_iota(jnp.int32, (g, cj_n), 0).astype(F32)
        onehot = jnp.where(grow == row, 1.0, 0.0).astype(BF16)
        wg = _dot(wt, onehot)
        pre = 64.0 * _dot(pre_hi, onehot)[0:1] + _dot(pre_lo, onehot)[0:1]
        jloc = j - pre
        lane = jnp.sum(jnp.where(wg <= jloc, 1.0, 0.0), axis=0, keepdims=True)
        idx_ref[:, jc * cj_n:(jc + 1) * cj_n] = (row * LANES + lane).astype(jnp.int32)
        ag = _dot(a_h, onehot) + _dot(a_m, onehot) + _dot(a_l, onehot)
        lrow = lax.broadcasted_iota(jnp.int32, (LANES, cj_n), 0).astype(F32)
        gate_ref[:, jc * cj_n:(jc + 1) * cj_n] = jnp.sum(jnp.where(lrow == lane, ag, 0.0), axis=0,
                                                          keepdims=True)


def _select(afft, cap):
    e, n = afft.shape
    g = n // LANES
    blk = lambda w: pl.BlockSpec((None, 1, w), lambda i: (i, 0, 0))
    tok = pl.BlockSpec((None, g, LANES), lambda i: (i, 0, 0))
    return pl.pallas_call(
        functools.partial(_select_kernel, cap=cap),
        grid=(e,),
        in_specs=[tok],
        out_specs=(blk(cap), blk(cap), tok, blk(g)),
        out_shape=(jax.ShapeDtypeStruct((e, 1, cap), jnp.int32), jax.ShapeDtypeStruct((e, 1, cap), F32),
                   jax.ShapeDtypeStruct((e, g, LANES), F32), jax.ShapeDtypeStruct((e, 1, g), F32)),
        compiler_params=_cparams("parallel"),
    )(afft.reshape(e, g, LANES))


FFN_TT = 512
FFN_SPLIT = 4
FFN_SLOTS = 3


def _ffn_kernel(idx_ref, h2_hbm, wgu_ref, wd_ref, gate_ref, ye_ref, xbuf, sem):
    tt = xbuf.shape[1] // H2_ROWS
    nj = pl.num_programs(1)
    step = pl.program_id(0) * nj + pl.program_id(1)
    last = pl.num_programs(0) * nj - 1
    nslot = xbuf.shape[0]
    ahead = nslot - 1
    slot = step % nslot

    def row_copy(tok, sl, r):
        return pltpu.make_async_copy(h2_hbm.at[pl.ds(pl.multiple_of(tok * H2_ROWS, H2_ROWS), H2_ROWS), :],
                                     xbuf.at[sl, pl.ds(r * H2_ROWS, H2_ROWS), :], sem.at[sl])

    def issue(s):
        sl = s % nslot
        for r in range(tt):
            row_copy(idx_ref[s * tt + r], sl, r).start()

    def wait_slot(sl):
        pltpu.make_async_copy(h2_hbm.at[pl.ds(0, tt * H2_ROWS), :], xbuf.at[sl], sem.at[sl]).wait()

    @pl.when(step == 0)
    def _():
        for s in range(ahead):
            @pl.when(s <= last)
            def _():
                issue(s)

    @pl.when(step + ahead <= last)
    def _():
        issue(step + ahead)

    wait_slot(slot)
    words = [xbuf[slot, pl.ds(c, tt, stride=H2_ROWS), :] for c in range(H2_ROWS)]
    x = jnp.concatenate([pltpu.bitcast(w & jnp.uint32(0xFFFF0000), F32).astype(BF16) for w in words]
                        + [pltpu.bitcast(w << 16, F32).astype(BF16) for w in words], axis=1)
    for part in range(FFN_SPLIT):
        r = slice(part * (tt // FFN_SPLIT), (part + 1) * (tt // FFN_SPLIT))
        au = _dot(x[r], wgu_ref[...])
        a = au[:, :D_EXPERT]
        u = au[:, D_EXPERT:]
        act = (a / (1.0 + jnp.exp(-a)) * u).astype(BF16)
        ye_ref[r, :] = (_dot(act, wd_ref[...]) * gate_ref[r, :]).astype(BF16)


def _ffn(idx_flat, h2p, wgu, wd, gate, cap):
    e, d = wgu.shape[0], wgu.shape[1]
    tt = _pick(cap, FFN_TT)
    return pl.pallas_call(
        _ffn_kernel,
        grid_spec=pltpu.PrefetchScalarGridSpec(
            num_scalar_prefetch=1,
            grid=(e, cap // tt),
            in_specs=[pl.BlockSpec(memory_space=pl.ANY),
                      pl.BlockSpec((None, d, 2 * D_EXPERT), lambda i, j, idx: (i, 0, 0)),
                      pl.BlockSpec((None, D_EXPERT, d), lambda i, j, idx: (i, 0, 0)),
                      pl.BlockSpec((None, tt, 1), lambda i, j, idx: (i, j, 0))],
            out_specs=pl.BlockSpec((None, tt, d), lambda i, j, idx: (i, j, 0)),
            scratch_shapes=[pltpu.VMEM((FFN_SLOTS, tt * H2_ROWS, LANES), jnp.uint32),
                            pltpu.SemaphoreType.DMA((FFN_SLOTS,))]),
        out_shape=jax.ShapeDtypeStruct((e, cap, d), BF16),
        compiler_params=_cparams("arbitrary", "arbitrary"),
    )(idx_flat, h2p, wgu, wd, gate)


CAST_ROWS = 1024
BF16_SUBLANES = 16


def _row_block(n, limit):
    return max(r for r in range(BF16_SUBLANES, limit + 1, BF16_SUBLANES) if n % r == 0)


def _cast_gu_kernel(g_ref, u_ref, o_ref):
    o_ref[:, :D_EXPERT] = g_ref[...].astype(BF16)
    o_ref[:, D_EXPERT:] = u_ref[...].astype(BF16)


def _cast_kernel(w_ref, o_ref):
    o_ref[...] = w_ref[...].astype(BF16)


def _cast_experts(w_gate, w_up, w_down):
    l, e, d, f = w_gate.shape
    blk = lambda r, c: pl.BlockSpec((None, None, r, c), lambda a, b, i: (a, b, i, 0))
    rg = _row_block(d, CAST_ROWS // 2)
    wgu = pl.pallas_call(
        _cast_gu_kernel, grid=(l, e, d // rg),
        in_specs=[blk(rg, f), blk(rg, f)], out_specs=blk(rg, 2 * f),
        out_shape=jax.ShapeDtypeStruct((l, e, d, 2 * f), BF16),
        compiler_params=_cparams("parallel", "parallel", "parallel"),
    )(w_gate, w_up)
    rd = _row_block(f, CAST_ROWS)
    wd = pl.pallas_call(
        _cast_kernel, grid=(l, e, f // rd),
        in_specs=[blk(rd, d)], out_specs=blk(rd, d),
        out_shape=jax.ShapeDtypeStruct((l, e, f, d), BF16),
        compiler_params=_cparams("parallel", "parallel", "parallel"),
    )(w_down)
    return wgu, wd


CMB_TT = 128
CMB_BLK = 16
CMB_KC = 256
CMB_ROWS = N_EXPERTS * CMB_TT + N_EXPERTS * 2 * CMB_BLK + CMB_KC


def _combine_kernel(rowpre_ref, x_ref, pos_ref, gfin_ref, ye_hbm, o_ref, buf, sem, acc_ref, *, g, final):
    i = pl.program_id(0)
    last = pl.num_programs(0) - 1
    slot = i % 2

    def plan(tile):
        starts, nblks, offs = [], [], []
        off = jnp.int32(0)
        for e in range(N_EXPERTS):
            lo = rowpre_ref[e * (g + 1) + tile]
            hi = rowpre_ref[e * (g + 1) + tile + 1]
            start = (lo >> 4) << 4
            nblk = jnp.where(hi > lo, (hi - start + (CMB_BLK - 1)) >> 4, 0)
            starts.append(start)
            nblks.append(nblk)
            offs.append(off)
            off = off + nblk
        return starts, nblks, offs, off

    def block_copy(e, src_row, dst_row, sl):
        return pltpu.make_async_copy(ye_hbm.at[e, pl.ds(src_row, CMB_BLK), :],
                                     buf.at[sl, pl.ds(dst_row, CMB_BLK), :], sem.at[sl])

    def issue(tile, sl):
        starts, nblks, offs, _ = plan(tile)
        for e in range(N_EXPERTS):
            def body(b, c, e=e):
                block_copy(e, pl.multiple_of(starts[e] + b * CMB_BLK, CMB_BLK),
                           pl.multiple_of((offs[e] + b) * CMB_BLK, CMB_BLK), sl).start()
                return c
            lax.fori_loop(0, nblks[e], body, 0)

    def wait(tile, sl):
        rows = plan(tile)[3] * CMB_BLK

        @pl.when(rows > 0)
        def _():
            pltpu.make_async_copy(ye_hbm.at[0, pl.ds(0, rows), :], buf.at[sl, pl.ds(0, rows), :],
                                  sem.at[sl]).wait()

    @pl.when(i == 0)
    def _():
        buf[...] = jnp.zeros_like(buf)
        issue(0, 0)

    @pl.when(i < last)
    def _():
        issue(i + 1, 1 - slot)

    wait(i, slot)
    starts, nblks, offs, total = plan(i)
    pos = jnp.concatenate([pos_ref[...], jnp.zeros((LANES - N_EXPERTS, CMB_TT), F32)], axis=0).T
    cols = []
    for e in range(N_EXPERTS):
        pe = pos[:, e:e + 1]
        shift = (offs[e] * CMB_BLK - starts[e]).astype(F32)
        col = jnp.where(pe >= 0.0, pe + shift, -1.0)
        cols.append(jnp.broadcast_to(col, (CMB_TT, CMB_KC)))
    acc_ref[...] = x_ref[...]

    def chunk(k, c):
        k0 = pl.multiple_of(k * CMB_KC, CMB_KC)
        r = (lax.broadcasted_iota(jnp.int32, (CMB_TT, CMB_KC), 1) + k0).astype(F32)
        hit = jnp.where(cols[0] == r, 1.0, 0.0)
        for e in range(1, N_EXPERTS):
            hit = hit + jnp.where(cols[e] == r, 1.0, 0.0)
        acc_ref[...] += _dot(hit.astype(BF16), buf[slot, pl.ds(k0, CMB_KC), :])
        return c

    lax.fori_loop(0, (total * CMB_BLK + (CMB_KC - 1)) // CMB_KC, chunk, 0)
    o_ref[...] = _rms(acc_ref[...], gfin_ref[...]) if final else acc_ref[...]


def _combine(rowpre_flat, x2, pos, g_final, ye, g, final):
    n, d = x2.shape
    assert n == g * CMB_TT
    return pl.pallas_call(
        functools.partial(_combine_kernel, g=g, final=final),
        grid_spec=pltpu.PrefetchScalarGridSpec(
            num_scalar_prefetch=1,
            grid=(g,),
            in_specs=[pl.BlockSpec((CMB_TT, d), lambda i, rp: (i, 0)),
                      pl.BlockSpec((N_EXPERTS, CMB_TT), lambda i, rp: (0, i)),
                      pl.BlockSpec((1, d), lambda i, rp: (0, 0)),
                      pl.BlockSpec(memory_space=pl.ANY)],
            out_specs=pl.BlockSpec((CMB_TT, d), lambda i, rp: (i, 0)),
            scratch_shapes=[pltpu.VMEM((2, CMB_ROWS, d), BF16), pltpu.SemaphoreType.DMA((2,)),
                            pltpu.VMEM((CMB_TT, d), F32)]),
        out_shape=jax.ShapeDtypeStruct((n, d), F32),
        compiler_params=_cparams("arbitrary"),
    )(rowpre_flat, x2, pos, g_final, ye)


def _rot_half_cols(w):
    half = w.shape[-1] // 2
    return jnp.concatenate([-w[..., half:], w[..., :half]], axis=-1)


def _prep_layer(l, g_attn, w_in, sink, g_cq, w_uq, g_ckv, w_uk, w_uv, g_ya, g_yb, w_out, g_ffn,
                w_router):
    wi = w_in[l]
    w_kr = wi[:, SRC_KR:SRC_KR + QK_ROPE]
    zpad = jnp.zeros((D_MODEL, 128 - QK_ROPE), F32)
    w_in_ext = jnp.concatenate([wi[:, SRC_KA:SRC_VA], wi[:, SRC_CQ:SRC_KR], w_kr, zpad,
                                _rot_half_cols(w_kr), zpad], axis=1)
    w_in_t = jnp.concatenate([wi[:, SRC_QA:SRC_KA], wi[:, SRC_VA:SRC_CQ]], axis=1).T
    wq = w_uq[l].reshape(Q_LORA, MLA_HEADS, QK_NOPE + QK_ROPE)
    w_uq_main = w_uq[l]
    w_uq_rot = _rot_half_cols(wq[:, :, QK_NOPE:]).reshape(Q_LORA, MLA_HEADS * QK_ROPE)
    return dict(
        g_attn=g_attn[l][None], w_in=w_in_ext.astype(BF16), w_in_t=w_in_t.astype(BF16), sink=sink[l],
        g_cq=g_cq[l][None], w_uq_main_t=w_uq_main.T.astype(BF16), w_uq_rot_t=w_uq_rot.T.astype(BF16),
        g_ckv=g_ckv[l][None], w_uk=w_uk[l].astype(BF16), w_uvt=w_uv[l].T.astype(BF16),
        g_ya=g_ya[l][None], g_yb=g_yb[l][None],
        w_out_a=w_out[l][:SWA_WIDTH].astype(BF16), w_out_b=w_out[l][SWA_WIDTH:].astype(BF16),
        g_ffn=g_ffn[l][None], w_router_t=w_router[l].T.astype(BF16), layer=l,
    )


def _rope_tables(s):
    inv = 1.0 / (ROPE_THETA ** (jnp.arange(0, QK_ROPE, 2, dtype=F32) / QK_ROPE))
    ang = jnp.arange(s, dtype=F32)[:, None] * inv[None, :]
    z = jnp.zeros((s, 128 - QK_ROPE), F32)
    cos, sin = jnp.cos(ang), jnp.sin(ang)
    cos128 = jnp.concatenate([cos, cos, z], axis=1)
    sin128 = jnp.concatenate([sin, sin, z], axis=1)
    return cos128, sin128, cos128.T, sin128.T


def _pick(n, pref):
    t = pref
    while n % t:
        t //= 2
    return t


def _layer(x, lw, experts, raw, tables, g_final, final):
    b, s, d = x.shape
    n = b * s
    qat, ka, vat, qmt, km, vt = _in_proj(x, lw, tables, _pick(s, MLA_TK))
    ya = _swa(qat, ka, vat, lw["sink"])
    if experts is None:
        yb, *experts = _mla(qmt, km, vt, raw, lw["layer"])
    else:
        yb = _mla(qmt, km, vt)
    x2, h2, afft = _out_proj(x.reshape(n, d), ya.reshape(n, SWA_WIDTH), yb.reshape(n, MLA_WIDTH), lw,
                             _pick(n, 512))
    cap = CAPACITY_FACTOR * n // N_EXPERTS
    g = n // LANES
    idx, gate, pos, rowpre = _select(afft, cap)
    ye = _ffn(idx.reshape(-1), h2, *experts, gate.reshape(N_EXPERTS, cap, 1), cap)
    bounds = jnp.concatenate([rowpre.reshape(N_EXPERTS, g), jnp.full((N_EXPERTS, 1), cap, F32)], axis=1)
    x2 = _combine(bounds.astype(jnp.int32).reshape(-1), x2, pos.reshape(N_EXPERTS, n), g_final, ye, g, final)
    return x2.reshape(b, s, d), experts


def _trunk(x, layers, experts, raw, g_final):
    b, s, d = x.shape
    tables = _rope_tables(s)
    produced = []
    for l, lw in enumerate(layers):
        x, ex = _layer(x, lw, None if experts is None else experts[l], raw, tables, g_final[None],
                       final=lw is layers[-1])
        produced.append(ex)
    return x, produced


def kernel(x_prompt, x_sample, g_attn, w_in, sink, g_cq, w_uq, g_ckv, w_uk, w_uv, g_ya, g_yb, w_out,
           g_ffn, w_router, w_gate, w_up, w_down, g_final):
    depth = w_in.shape[0]
    layers = [_prep_layer(l, g_attn, w_in, sink, g_cq, w_uq, g_ckv, w_uk, w_uv, g_ya, g_yb, w_out,
                          g_ffn, w_router) for l in range(depth)]
    raw = (w_gate, w_up, w_down)
    b, s, _ = x_prompt.shape
    steps = b * MLA_HEADS * (s // (MLA_NSUB * MLA_TQ)) if s % (MLA_NSUB * MLA_TQ) == 0 else 0
    experts = None
    if not steps or _cast_plan(steps, *w_gate.shape[1:]) is None:
        wgu, wd = _cast_experts(*raw)
        experts = [(wgu[l], wd[l]) for l in range(depth)]
    y_prompt, experts = _trunk(x_prompt, layers, experts, raw, g_final)
    y_sample, _ = _trunk(x_sample, layers, experts, raw, g_final)
    return y_prompt, y_sample
```
